```python
import jax, jax.numpy as jnp
from jax import lax
import numpy as np

D_MODEL = 1024
BATCH = 4
SEQ = 8192
DEPTH = 2

N_HEADS = 8
N_KV_HEADS = 2
HEAD_DIM = 64
ATTN_WIDTH = N_HEADS * HEAD_DIM
KV_WIDTH = N_KV_HEADS * HEAD_DIM
WINDOW = 128
ROT_DIM = HEAD_DIM // 4
ROPE_THETA = 500000.0
HGRN_HEADS = 4
HGRN_HEAD_DIM = 128
HGRN_WIDTH = HGRN_HEADS * HGRN_HEAD_DIM
CHUNK = 64
NORM_EPS = 1e-6
MASK_VALUE = -1e30
SPLITS = (ATTN_WIDTH, KV_WIDTH, KV_WIDTH, ATTN_WIDTH,
          HGRN_WIDTH, HGRN_WIDTH, HGRN_WIDTH, HGRN_WIDTH,
          D_MODEL, D_MODEL)
IN_WIDTH = 2 * ATTN_WIDTH + 2 * KV_WIDTH + 4 * HGRN_WIDTH + 2 * D_MODEL

kernel_name = 'hybrid_swa_sink_hgrn2_gated_merge'


def rms_norm(x, w):
    xf = x.astype(jnp.float32)
    y = xf * lax.rsqrt(jnp.mean(xf * xf, axis=-1, keepdims=True) + NORM_EPS)
    return (y * w.astype(jnp.float32)).astype(x.dtype)


def partial_rope(x, positions):
    half = ROT_DIM // 2
    inv_freq = jnp.power(ROPE_THETA, -jnp.arange(half, dtype=jnp.float32) * (2.0 / ROT_DIM))
    ang = positions.astype(jnp.float32)[..., None] * inv_freq
    cos = jnp.cos(ang)[:, :, None, :]
    sin = jnp.sin(ang)[:, :, None, :]
    xr = x[..., :ROT_DIM].astype(jnp.float32)
    x1, x2 = xr[..., :half], xr[..., half:]
    rot = jnp.concatenate([x1 * cos - x2 * sin, x2 * cos + x1 * sin], axis=-1)
    return jnp.concatenate([rot.astype(x.dtype), x[..., ROT_DIM:]], axis=-1)


def sliding_window_attention(q, k, v, sinks):
    B, T = q.shape[0], q.shape[1]
    nb = T // WINDOW
    G = N_HEADS // N_KV_HEADS
    qb = q.astype(jnp.float32).reshape(B, nb, WINDOW, N_KV_HEADS, G, HEAD_DIM) * (HEAD_DIM ** -0.5)

    def band(a):
        a = a.astype(jnp.float32).reshape(B, nb, WINDOW, N_KV_HEADS, HEAD_DIM)
        prev = jnp.pad(a[:, :-1], ((0, 0), (1, 0), (0, 0), (0, 0), (0, 0)))
        return jnp.concatenate([prev, a], axis=2)

    kb, vb = band(k), band(v)
    blk = jnp.arange(nb)[:, None]
    qpos = blk * WINDOW + jnp.arange(WINDOW)[None, :]
    kpos = (blk - 1) * WINDOW + jnp.arange(2 * WINDOW)[None, :]
    delta = qpos[:, :, None] - kpos[:, None, :]
    mask = (delta >= 0) & (delta < WINDOW) & (kpos[:, None, :] >= 0)
    s = jnp.einsum('bnqhgd,bnkhd->bhgnqk', qb, kb)
    s = jnp.where(mask, s, MASK_VALUE)
    sink = sinks.astype(jnp.float32).reshape(N_KV_HEADS, G)[None, :, :, None, None, None]
    m = jnp.maximum(jnp.max(s, axis=-1, keepdims=True), sink)
    p = jnp.exp(s - m)
    p = p / (jnp.sum(p, axis=-1, keepdims=True) + jnp.exp(sink - m))
    o = jnp.einsum('bhgnqk,bnkhd->bnqhgd', p, vb)
    return o.reshape(B, T, ATTN_WIDTH)


def hgrn2_recurrence(q, k, v, logf):
    B, T, H, Dk = q.shape
    Dv = v.shape[-1]
    n = T // CHUNK

    def chunks(a):
        return a.reshape(B, n, CHUNK, H, a.shape[-1]).transpose(1, 0, 3, 2, 4)

    causal = jnp.tril(jnp.ones((CHUNK, CHUNK), dtype=bool))[:, :, None]

    def step(S, blk):
        qc, kc, vc, gc = blk
        G = jnp.cumsum(gc, axis=-2)
        diff = G[:, :, :, None, :] - G[:, :, None, :, :]
        decay = jnp.exp(jnp.where(causal, diff, MASK_VALUE))
        scores = jnp.einsum('bhtd,bhsd,bhtsd->bhts', qc, kc, decay)
        o = (jnp.einsum('bhts,bhsv->bhtv', scores, vc)
             + jnp.einsum('bhtd,bhdv->bhtv', qc * jnp.exp(G), S))
        G_end = G[:, :, -1:, :]
        S = (S * jnp.exp(G_end)[:, :, 0, :, None]
             + jnp.einsum('bhsd,bhsv->bhdv', kc * jnp.exp(G_end - G), vc))
        return S, o

    S0 = jnp.zeros((B, H, Dk, Dv), jnp.float32)
    _, o = lax.scan(step, S0, (chunks(q), chunks(k), chunks(v), chunks(logf)))
    return o.transpose(1, 0, 3, 2, 4).reshape(B, T, H, Dv)


def setup_inputs(seed: int = 0) -> dict:
    key = jax.random.key(seed)
    ks = jax.random.split(key, 12)
    f32 = jnp.float32
    x = jax.random.normal(ks[0], (BATCH, SEQ, D_MODEL), f32)
    offsets = jax.random.randint(ks[1], (BATCH, 1), 0, 1024, dtype=jnp.int32)
    positions = (jnp.arange(SEQ, dtype=jnp.int32)[None, :] + offsets).astype(jnp.int32)
    norm_w = 1.0 + 0.02 * jax.random.normal(ks[2], (DEPTH, D_MODEL), f32)
    w_in = jax.random.normal(ks[3], (DEPTH, D_MODEL, IN_WIDTH), f32) * D_MODEL ** -0.5
    attn_sinks = 0.5 * jax.random.normal(ks[4], (DEPTH, N_HEADS), f32)
    hgrn_norm_w = 1.0 + 0.02 * jax.random.normal(ks[5], (DEPTH, HGRN_HEAD_DIM), f32)
    w_up_attn = jax.random.normal(ks[6], (DEPTH, ATTN_WIDTH, D_MODEL), f32) * ATTN_WIDTH ** -0.5
    w_up_hgrn = jax.random.normal(ks[7], (DEPTH, HGRN_WIDTH, D_MODEL), f32) * HGRN_WIDTH ** -0.5
    w_out = jax.random.normal(ks[8], (DEPTH, D_MODEL, D_MODEL), f32) * D_MODEL ** -0.5
    lb_logits = 0.1 * jax.random.normal(ks[9], (DEPTH, HGRN_WIDTH), f32)
    final_norm_w = 1.0 + 0.02 * jax.random.normal(ks[10], (D_MODEL,), f32)
    return {'x': x, 'positions': positions, 'norm_w': norm_w, 'w_in': w_in,
            'attn_sinks': attn_sinks, 'hgrn_norm_w': hgrn_norm_w, 'w_up_attn': w_up_attn,
            'w_up_hgrn': w_up_hgrn, 'w_out': w_out, 'lb_logits': lb_logits,
            'final_norm_w': final_norm_w}


def reference(x, positions, norm_w, w_in, attn_sinks, hgrn_norm_w, w_up_attn, w_up_hgrn,
              w_out, lb_logits, final_norm_w):
    B, T, _ = x.shape
    split_points = [int(s) for s in np.cumsum(SPLITS)[:-1]]
    lb_soft = jax.nn.softmax(lb_logits.astype(jnp.float32), axis=0)
    lb_all = jnp.cumsum(lb_soft, axis=0) - lb_soft[0]

    for layer in range(DEPTH):
        h = rms_norm(x, norm_w[layer])
        proj = h @ w_in[layer]
        (q, k, v, z_attn, q_h, f_h, i_h, g_h, gate_attn, gate_hgrn) = jnp.split(proj, split_points, axis=-1)

        q = partial_rope(q.reshape(B, T, N_HEADS, HEAD_DIM), positions)
        k = partial_rope(k.reshape(B, T, N_KV_HEADS, HEAD_DIM), positions)
        v = v.reshape(B, T, N_KV_HEADS, HEAD_DIM)
        a = sliding_window_attention(q, k, v, attn_sinks[layer]).astype(x.dtype) * jax.nn.silu(z_attn)

        lb = lb_all[layer]
        fx = f_h.astype(jnp.float32)
        logf = jnp.log(lb + (1.0 - lb) * jax.nn.sigmoid(fx))
        kin = (1.0 - lb) * jax.nn.sigmoid(-fx)
        qin = jax.nn.silu(q_h.astype(jnp.float32))
        heads = lambda t: t.reshape(B, T, HGRN_HEADS, HGRN_HEAD_DIM)
        o = hgrn2_recurrence(heads(qin), heads(kin), heads(i_h.astype(jnp.float32)), heads(logf))
        o = o * lax.rsqrt(jnp.mean(o * o, axis=-1, keepdims=True) + NORM_EPS) * hgrn_norm_w[layer].astype(jnp.float32)
        b = o.reshape(B, T, HGRN_WIDTH).astype(x.dtype) * jax.nn.silu(g_h)

        merged = (jax.nn.sigmoid(gate_attn) * (a @ w_up_attn[layer])
                  + jax.nn.sigmoid(gate_hgrn) * (b @ w_up_hgrn[layer]))
        x = x + merged @ w_out[layer]

    return rms_norm(x, final_norm_w)
```

```python
import functools

import numpy as np
import jax
import jax.numpy as jnp
from jax import lax
from jax.experimental import pallas as pl
from jax.experimental.pallas import tpu as pltpu

D_MODEL = 1024
DEPTH = 2
N_HEADS = 8
N_KV_HEADS = 2
HEAD_DIM = 64
ATTN_WIDTH = N_HEADS * HEAD_DIM
KV_WIDTH = N_KV_HEADS * HEAD_DIM
WINDOW = 128
ROT_DIM = HEAD_DIM // 4
ROPE_THETA = 500000.0
HGRN_HEADS = 4
HGRN_HEAD_DIM = 128
HGRN_WIDTH = HGRN_HEADS * HGRN_HEAD_DIM
CHUNK = 64
NORM_EPS = 1e-6
MASK_VALUE = -1e30
IN_WIDTH = 2 * ATTN_WIDTH + 2 * KV_WIDTH + 4 * HGRN_WIDTH + 2 * D_MODEL

OFF_Q = 0
OFF_K = OFF_Q + ATTN_WIDTH
OFF_V = OFF_K + KV_WIDTH
OFF_Z = OFF_V + KV_WIDTH
OFF_HQ = OFF_Z + ATTN_WIDTH
OFF_HF = OFF_HQ + HGRN_WIDTH
OFF_HI = OFF_HF + HGRN_WIDTH
OFF_HG = OFF_HI + HGRN_WIDTH
OFF_GA = OFF_HG + HGRN_WIDTH
OFF_GH = OFF_GA + D_MODEL

LANES = 128
TOKEN_BLOCK = 256
ROPE_BLOCK = 512
VMEM_LIMIT_BYTES = 56 * 1024 * 1024

LEVELS = (32, 16, 8, 4, 2, 1)
N_SUMS = len(LEVELS) + 2

F32 = jnp.float32
BF16 = jnp.bfloat16


def _dot_nn(a, b):
    return lax.dot_general(a, b, (((1,), (0,)), ((), ())), preferred_element_type=F32)


def _dot_nt(a, b):
    return lax.dot_general(a, b, (((1,), (1,)), ((), ())), preferred_element_type=F32)


def _dot_tn(a, b):
    return lax.dot_general(a, b, (((0,), (0,)), ((), ())), preferred_element_type=F32)


def _split3(x):
    hi = x.astype(BF16)
    r1 = x - hi.astype(F32)
    mid = r1.astype(BF16)
    lo = (r1 - mid.astype(F32)).astype(BF16)
    return hi, mid, lo


def _sigmoid(x):
    return 1.0 / (1.0 + jnp.exp(-x))


def _hgrn_constants():
    t = np.arange(CHUNK)[:, None]
    u = np.arange(CHUNK)[None, :]
    sums, masks = [], []
    for h in LEVELS:
        right = (t // h) % 2 == 1
        m = (t // (2 * h)) * 2 * h + h - 1
        sums.append(np.where(right, (u > m) & (u <= t), (u > t) & (u <= m)))
        masks.append((t // (2 * h) == u // (2 * h)) & right & ((u // h) % 2 == 0))
    sums.append(u <= t)
    sums.append(u > t)
    masks.append(t == u)
    return (np.concatenate(sums, axis=0).astype(np.float32),
            np.stack(masks, axis=0).astype(np.float32))


def _rope_expand_matrix():
    half = ROT_DIM // 2
    e = np.zeros((2 * half, 2 * LANES), np.float32)
    for lane in range(LANES):
        d = lane % HEAD_DIM
        if d < half:
            e[d, lane] = 1.0
            e[half + d, LANES + lane] = -1.0
        elif d < ROT_DIM:
            e[d - half, lane] = 1.0
            e[half + d - half, LANES + lane] = 1.0
    return e


def _rope_table_kernel(pos_ref, invf_ref, expand_ref, cos_ref, sin_ref):
    batch = pos_ref.shape[0]
    lane = lax.broadcasted_iota(jnp.int32, (1, LANES), 1) % HEAD_DIM
    passthrough = jnp.where(lane >= ROT_DIM, 1.0, 0.0).astype(F32)
    expand = expand_ref[...]
    for b in range(batch):
        pos = pos_ref[b:b + 1, :].astype(F32)
        ang = invf_ref[...] * pos
        cs = jnp.concatenate([jnp.cos(ang), jnp.sin(ang)], axis=0)
        out = None
        for piece in _split3(cs):
            part = _dot_tn(piece, expand)
            out = part if out is None else out + part
        cos_ref[b] = out[:, :LANES] + passthrough
        sin_ref[b] = out[:, LANES:]


def _rope_tables(positions):
    batch, seq = positions.shape
    half = ROT_DIM // 2
    inv_freq = jnp.power(ROPE_THETA, -jnp.arange(half, dtype=F32) * (2.0 / ROT_DIM)).reshape(half, 1)
    expand = jnp.asarray(_rope_expand_matrix(), dtype=BF16)
    out_sds = jax.ShapeDtypeStruct((batch, seq, LANES), F32)
    return pl.pallas_call(
        _rope_table_kernel,
        grid=(seq // ROPE_BLOCK,),
        in_specs=[
            pl.BlockSpec((batch, ROPE_BLOCK), lambda t: (0, t)),
            pl.BlockSpec((half, 1), lambda t: (0, 0)),
            pl.BlockSpec((2 * half, 2 * LANES), lambda t: (0, 0)),
        ],
        out_specs=[
            pl.BlockSpec((batch, ROPE_BLOCK, LANES), lambda t: (0, t, 0)),
            pl.BlockSpec((batch, ROPE_BLOCK, LANES), lambda t: (0, t, 0)),
        ],
        out_shape=[out_sds, out_sds],
        name="rope_tables",
    )(positions, inv_freq, expand)


def _rms_norm(x, w):
    return x * lax.rsqrt(jnp.mean(x * x, axis=-1, keepdims=True) + NORM_EPS) * w


def _layer_kernel(sinks_ref, x_ref, cos_ref, sin_ref, nw_ref, win_ref, hnw_ref, wua_ref, wuh_ref,
                  wo_ref, lbl_ref, sums_ref, masks_ref, fnw_ref, o_ref,
                  proj_s, q_s, kb_s, vb_s, st_s, a_s, b_s, *, layer, final):
    tb = x_ref.shape[1]
    t_idx = pl.program_id(1)

    @pl.when(t_idx == 0)
    def _():
        kb_s[0:WINDOW, :] = jnp.zeros((WINDOW, KV_WIDTH), kb_s.dtype)
        vb_s[0:WINDOW, :] = jnp.zeros((WINDOW, KV_WIDTH), vb_s.dtype)
        st_s[...] = jnp.zeros(st_s.shape, st_s.dtype)

    x = x_ref[0]
    h = _rms_norm(x, nw_ref[...]).astype(BF16)
    proj_s[...] = _dot_nn(h, win_ref[...])

    half = ROT_DIM // 2
    lane = lax.broadcasted_iota(jnp.int32, (1, LANES), 1) % HEAD_DIM
    first_half = lane < half
    cos_t = cos_ref[0]
    sin_t = sin_ref[0]

    def rope(tile):
        partner = jnp.where(first_half, pltpu.roll(tile, LANES - half, axis=1), pltpu.roll(tile, half, axis=1))
        return tile * cos_t + partner * sin_t

    scale = HEAD_DIM ** -0.5
    for j in range(ATTN_WIDTH // LANES):
        q_s[:, j * LANES:(j + 1) * LANES] = (
            rope(proj_s[:, OFF_Q + j * LANES:OFF_Q + (j + 1) * LANES]) * scale).astype(q_s.dtype)
    for j in range(KV_WIDTH // LANES):
        kb_s[WINDOW:, j * LANES:(j + 1) * LANES] = rope(
            proj_s[:, OFF_K + j * LANES:OFF_K + (j + 1) * LANES]).astype(kb_s.dtype)
    vb_s[WINDOW:, :] = proj_s[:, OFF_V:OFF_V + KV_WIDTH].astype(vb_s.dtype)

    row = lax.broadcasted_iota(jnp.int32, (WINDOW, 2 * WINDOW), 0)
    col = lax.broadcasted_iota(jnp.int32, (WINDOW, 2 * WINDOW), 1)
    band = (col > row) & (col <= row + WINDOW)
    group = N_HEADS // N_KV_HEADS
    for i in range(tb // WINDOW):
        has_prev = jnp.logical_or(t_idx > 0, i > 0)
        mask = band & jnp.logical_or(col >= WINDOW, has_prev)
        outs = []
        for head in range(N_HEADS):
            kv = head // group
            qh = q_s[i * WINDOW:(i + 1) * WINDOW, head * HEAD_DIM:(head + 1) * HEAD_DIM]
            kh = kb_s[i * WINDOW:(i + 2) * WINDOW, kv * HEAD_DIM:(kv + 1) * HEAD_DIM]
            vh = vb_s[i * WINDOW:(i + 2) * WINDOW, kv * HEAD_DIM:(kv + 1) * HEAD_DIM]
            s = jnp.where(mask, _dot_nt(qh, kh), MASK_VALUE)
            sink = sinks_ref[head]
            m = jnp.maximum(jnp.max(s, axis=-1, keepdims=True), sink)
            p = jnp.exp(s - m)
            denom = jnp.sum(p, axis=-1, keepdims=True) + jnp.exp(sink - m)
            outs.append(_dot_nn(p.astype(BF16), vh) / denom)
        attn = jnp.concatenate(outs, axis=1)
        z = proj_s[i * WINDOW:(i + 1) * WINDOW, OFF_Z:OFF_Z + ATTN_WIDTH]
        a_s[i * WINDOW:(i + 1) * WINDOW, :] = (attn * (z * _sigmoid(z))).astype(a_s.dtype)
    kb_s[0:WINDOW, :] = kb_s[tb:tb + WINDOW, :]
    vb_s[0:WINDOW, :] = vb_s[tb:tb + WINDOW, :]

    lbl = lbl_ref[...]
    lb_e = jnp.exp(lbl - jnp.max(lbl, axis=0, keepdims=True))
    lb_soft = lb_e / jnp.sum(lb_e, axis=0, keepdims=True)
    lb = jnp.sum(lb_soft[0:layer + 1, :], axis=0, keepdims=True) - lb_soft[0:1, :]
    one_minus_lb = 1.0 - lb
    hnw = hnw_ref[...]

    def chunk_body(c, carry):
        rows = pl.ds(pl.multiple_of(c * CHUNK, CHUNK), CHUNK)
        fx = proj_s[rows, OFF_HF:OFF_HF + HGRN_WIDTH]
        e = jnp.exp(-jnp.abs(fx))
        r = 1.0 / (1.0 + e)
        er = e * r
        pos_f = fx >= 0
        logf = jnp.log(lb + one_minus_lb * jnp.where(pos_f, r, er))
        kin = one_minus_lb * jnp.where(pos_f, er, r)
        qx = proj_s[rows, OFF_HQ:OFF_HQ + HGRN_WIDTH]
        qin = qx * _sigmoid(qx)
        vin = proj_s[rows, OFF_HI:OFF_HI + HGRN_WIDTH]
        gx = proj_s[rows, OFF_HG:OFF_HG + HGRN_WIDTH]
        gate = gx * _sigmoid(gx)

        sums = sums_ref[...]
        dsum = None
        for piece in _split3(logf):
            part = _dot_nn(sums, piece)
            dsum = part if dsum is None else dsum + part
        decay = jnp.exp(dsum)

        for head in range(HGRN_HEADS):
            hs = slice(head * HGRN_HEAD_DIM, (head + 1) * HGRN_HEAD_DIM)
            q = qin[:, hs]
            k = kin[:, hs]
            v = vin[:, hs].astype(BF16)
            scores = masks_ref[len(LEVELS)] * _dot_nt(q.astype(BF16), k.astype(BF16))
            for li in range(len(LEVELS)):
                d_l = decay[li * CHUNK:(li + 1) * CHUNK, hs]
                scores = scores + masks_ref[li] * _dot_nt((q * d_l).astype(BF16), (k * d_l).astype(BF16))
            g_cum = decay[len(LEVELS) * CHUNK:(len(LEVELS) + 1) * CHUNK, hs]
            g_rest = decay[(len(LEVELS) + 1) * CHUNK:(len(LEVELS) + 2) * CHUNK, hs]
            state = st_s[head]
            o = _dot_nn(scores.astype(BF16), v) + _dot_nt((q * g_cum).astype(BF16), state.astype(BF16))
            st_s[head] = state * g_cum[CHUNK - 1:CHUNK, :] + _dot_tn(v, (k * g_rest).astype(BF16))
            o = o * lax.rsqrt(jnp.mean(o * o, axis=-1, keepdims=True) + NORM_EPS) * hnw
            b_s[rows, hs] = (o * gate[:, hs]).astype(b_s.dtype)
        return carry

    lax.fori_loop(0, tb // CHUNK, chunk_body, 0)

    up_a = _dot_nn(a_s[...], wua_ref[...])
    up_h = _dot_nn(b_s[...], wuh_ref[...])
    merged = (_sigmoid(proj_s[:, OFF_GA:OFF_GA + D_MODEL]) * up_a
              + _sigmoid(proj_s[:, OFF_GH:OFF_GH + D_MODEL]) * up_h)
    y = x + _dot_nn(merged.astype(BF16), wo_ref[...])
    if final:
        y = _rms_norm(y, fnw_ref[...])
    o_ref[0] = y


def _layer_call(x, cos_tab, sin_tab, sinks, norm_w, w_in, hgrn_norm_w, w_up_attn, w_up_hgrn, w_out,
                lb_logits, sums, masks, final_norm_w, *, layer, final):
    batch, seq, d = x.shape
    tb = TOKEN_BLOCK
    const2 = lambda b, t: (0, 0)
    const3 = lambda b, t: (0, 0, 0)
    tok = lambda b, t: (b, t, 0)
    return pl.pallas_call(
        functools.partial(_layer_kernel, layer=layer, final=final),
        grid=(batch, seq // tb),
        in_specs=[
            pl.BlockSpec(memory_space=pltpu.SMEM),
            pl.BlockSpec((1, tb, d), tok),
            pl.BlockSpec((1, tb, LANES), tok),
            pl.BlockSpec((1, tb, LANES), tok),
            pl.BlockSpec((1, d), const2),
            pl.BlockSpec((d, IN_WIDTH), const2),
            pl.BlockSpec((1, HGRN_HEAD_DIM), const2),
            pl.BlockSpec((ATTN_WIDTH, d), const2),
            pl.BlockSpec((HGRN_WIDTH, d), const2),
            pl.BlockSpec((d, d), const2),
            pl.BlockSpec((DEPTH, HGRN_WIDTH), const2),
            pl.BlockSpec((N_SUMS * CHUNK, CHUNK), const2),
            pl.BlockSpec((len(LEVELS) + 1, CHUNK, CHUNK), const3),
            pl.BlockSpec((1, d), const2),
        ],
        out_specs=pl.BlockSpec((1, tb, d), tok),
        out_shape=jax.ShapeDtypeStruct(x.shape, x.dtype),
        scratch_shapes=[
            pltpu.VMEM((tb, IN_WIDTH), F32),
            pltpu.VMEM((tb, ATTN_WIDTH), BF16),
            pltpu.VMEM((WINDOW + tb, KV_WIDTH), BF16),
            pltpu.VMEM((WINDOW + tb, KV_WIDTH), BF16),
            pltpu.VMEM((HGRN_HEADS, HGRN_HEAD_DIM, HGRN_HEAD_DIM), F32),
            pltpu.VMEM((tb, ATTN_WIDTH), BF16),
            pltpu.VMEM((tb, HGRN_WIDTH), BF16),
        ],
        compiler_params=pltpu.CompilerParams(
            dimension_semantics=("arbitrary", "arbitrary"),
            vmem_limit_bytes=VMEM_LIMIT_BYTES),
        name=f"hybrid_layer_{layer}",
    )(sinks, x, cos_tab, sin_tab, norm_w, w_in, hgrn_norm_w, w_up_attn, w_up_hgrn, w_out,
      lb_logits, sums, masks, final_norm_w)


def kernel(x, positions, norm_w, w_in, attn_sinks, hgrn_norm_w, w_up_attn, w_up_hgrn, w_out, lb_logits,
           final_norm_w):
    depth = w_in.shape[0]
    assert depth == DEPTH and x.shape[1] % TOKEN_BLOCK == 0 and x.shape[1] % ROPE_BLOCK == 0
    cos_tab, sin_tab = _rope_tables(positions)
    sums_np, masks_np = _hgrn_constants()
    sums = jnp.asarray(sums_np, dtype=BF16)
    masks = jnp.asarray(masks_np, dtype=F32)
    fnw = final_norm_w.reshape(1, D_MODEL)
    for layer in range(depth):
        x = _layer_call(
            x, cos_tab, sin_tab, attn_sinks[layer], norm_w[layer].reshape(1, D_MODEL),
            w_in[layer].astype(BF16), hgrn_norm_w[layer].reshape(1, HGRN_HEAD_DIM),
            w_up_attn[layer].astype(BF16), w_up_hgrn[layer].astype(BF16), w_out[layer].astype(BF16),
            lb_logits, sums, masks, fnw, layer=layer, final=(layer == depth - 1))
    return x
```

```python
import functools

import numpy as np
import jax
import jax.numpy as jnp
from jax import lax
from jax.experimental import pallas as pl
from jax.experimental.pallas import tpu as pltpu

D_MODEL = 1024
DEPTH = 2
N_HEADS = 8
N_KV_HEADS = 2
HEAD_DIM = 64
ATTN_WIDTH = N_HEADS * HEAD_DIM
KV_WIDTH = N_KV_HEADS * HEAD_DIM
WINDOW = 128
ROT_DIM = HEAD_DIM // 4
ROPE_THETA = 500000.0
HGRN_HEADS = 4
HGRN_HEAD_DIM = 128
HGRN_WIDTH = HGRN_HEADS * HGRN_HEAD_DIM
CHUNK = 64
NORM_EPS = 1e-6
MASK_VALUE = -1e30
IN_WIDTH = 2 * ATTN_WIDTH + 2 * KV_WIDTH + 4 * HGRN_WIDTH + 2 * D_MODEL

OFF_Q = 0
OFF_K = OFF_Q + ATTN_WIDTH
OFF_V = OFF_K + KV_WIDTH
OFF_Z = OFF_V + KV_WIDTH
OFF_HQ = OFF_Z + ATTN_WIDTH
OFF_HF = OFF_HQ + HGRN_WIDTH
OFF_HI = OFF_HF + HGRN_WIDTH
OFF_HG = OFF_HI + HGRN_WIDTH
OFF_GA = OFF_HG + HGRN_WIDTH
OFF_GH = OFF_GA + D_MODEL

LANES = 128
TOKEN_BLOCK = 256
ROPE_BLOCK = 512
VMEM_LIMIT_BYTES = 56 * 1024 * 1024

LEVELS = (32, 16, 8, 4, 2, 1)
N_SUMS = len(LEVELS) + 2

F32 = jnp.float32
BF16 = jnp.bfloat16


def _dot_nn(a, b):
    return lax.dot_general(a, b, (((1,), (0,)), ((), ())), preferred_element_type=F32)


def _dot_nt(a, b):
    return lax.dot_general(a, b, (((1,), (1,)), ((), ())), preferred_element_type=F32)


def _dot_tn(a, b):
    return lax.dot_general(a, b, (((0,), (0,)), ((), ())), preferred_element_type=F32)


def _split3(x):
    hi = x.astype(BF16)
    r1 = x - hi.astype(F32)
    mid = r1.astype(BF16)
    lo = (r1 - mid.astype(F32)).astype(BF16)
    return hi, mid, lo


def _sigmoid(x):
    return 0.5 * jnp.tanh(0.5 * x) + 0.5


def _block_diag2(a):
    zero = jnp.zeros((a.shape[0], LANES), a.dtype)
    top = jnp.concatenate([a[:, :LANES], zero], axis=1)
    bottom = jnp.concatenate([zero, a[:, LANES:]], axis=1)
    return jnp.concatenate([top, bottom], axis=0)


def _hgrn_constants():
    t = np.arange(CHUNK)[:, None]
    u = np.arange(CHUNK)[None, :]
    sums, masks = [], []
    for h in LEVELS:
        right = (t // h) % 2 == 1
        m = (t // (2 * h)) * 2 * h + h - 1
        sums.append(np.where(right, (u > m) & (u <= t), (u > t) & (u <= m)))
        masks.append((t // (2 * h) == u // (2 * h)) & right & ((u // h) % 2 == 0))
    sums.append(u <= t)
    sums.append(u > t)
    masks.append(t == u)
    sums = np.concatenate(sums, axis=0).astype(np.float32)
    masks = np.stack(masks, axis=0).astype(np.float32)
    return np.concatenate([sums] * 3, axis=1), np.concatenate([masks] * 2, axis=2)


def _rope_expand_matrix():
    half = ROT_DIM // 2
    e = np.zeros((2 * half, 2 * LANES), np.float32)
    for lane in range(LANES):
        d = lane % HEAD_DIM
        if d < half:
            e[d, lane] = 1.0
            e[half + d, LANES + lane] = -1.0
        elif d < ROT_DIM:
            e[d - half, lane] = 1.0
            e[half + d - half, LANES + lane] = 1.0
    return e


def _rope_table_kernel(pos_ref, invf_ref, expand_ref, cos_ref, sin_ref):
    batch = pos_ref.shape[0]
    lane = lax.broadcasted_iota(jnp.int32, (1, LANES), 1) % HEAD_DIM
    passthrough = jnp.where(lane >= ROT_DIM, 1.0, 0.0).astype(F32)
    expand = expand_ref[...]
    for b in range(batch):
        pos = pos_ref[b:b + 1, :].astype(F32)
        ang = invf_ref[...] * pos
        cs = jnp.concatenate([jnp.cos(ang), jnp.sin(ang)], axis=0)
        out = None
        for piece in _split3(cs):
            part = _dot_tn(piece, expand)
            out = part if out is None else out + part
        cos_ref[b] = out[:, :LANES] + passthrough
        sin_ref[b] = out[:, LANES:]


def _rope_tables(positions):
    batch, seq = positions.shape
    half = ROT_DIM // 2
    inv_freq = jnp.power(ROPE_THETA, -jnp.arange(half, dtype=F32) * (2.0 / ROT_DIM)).reshape(half, 1)
    expand = jnp.asarray(_rope_expand_matrix(), dtype=BF16)
    out_sds = jax.ShapeDtypeStruct((batch, seq, LANES), F32)
    return pl.pallas_call(
        _rope_table_kernel,
        grid=(seq // ROPE_BLOCK,),
        in_specs=[
            pl.BlockSpec((batch, ROPE_BLOCK), lambda t: (0, t)),
            pl.BlockSpec((half, 1), lambda t: (0, 0)),
            pl.BlockSpec((2 * half, 2 * LANES), lambda t: (0, 0)),
        ],
        out_specs=[
            pl.BlockSpec((batch, ROPE_BLOCK, LANES), lambda t: (0, t, 0)),
            pl.BlockSpec((batch, ROPE_BLOCK, LANES), lambda t: (0, t, 0)),
        ],
        out_shape=[out_sds, out_sds],
        name="rope_tables",
    )(positions, inv_freq, expand)


def _rms_norm(x, w):
    return x * lax.rsqrt(jnp.mean(x * x, axis=-1, keepdims=True) + NORM_EPS) * w


def _layer_kernel(sinks_ref, x_ref, cos_ref, sin_ref, nw_ref, win_ref, hnw_ref, wua_ref, wuh_ref,
                  wo_ref, lbl_ref, sums_ref, masks_ref, fnw_ref, o_ref,
                  proj_s, q_s, kb_s, vb_s, st_s, a_s, b_s, *, layer, final):
    tb = x_ref.shape[1]
    t_idx = pl.program_id(1)

    @pl.when(t_idx == 0)
    def _():
        kb_s[0:WINDOW, :] = jnp.zeros((WINDOW, KV_WIDTH), kb_s.dtype)
        vb_s[0:WINDOW, :] = jnp.zeros((WINDOW, KV_WIDTH), vb_s.dtype)
        st_s[...] = jnp.zeros(st_s.shape, st_s.dtype)

    x = x_ref[0]
    h = _rms_norm(x, nw_ref[...]).astype(BF16)
    proj_s[...] = _dot_nn(h, win_ref[...])

    half = ROT_DIM // 2
    lane = lax.broadcasted_iota(jnp.int32, (1, LANES), 1) % HEAD_DIM
    first_half = lane < half
    cos_t = cos_ref[0]
    sin_t = sin_ref[0]

    def rope(tile):
        partner = jnp.where(first_half, pltpu.roll(tile, LANES - half, axis=1), pltpu.roll(tile, half, axis=1))
        return tile * cos_t + partner * sin_t

    scale = HEAD_DIM ** -0.5
    for j in range(ATTN_WIDTH // LANES):
        q_s[:, j * LANES:(j + 1) * LANES] = (
            rope(proj_s[:, OFF_Q + j * LANES:OFF_Q + (j + 1) * LANES]) * scale).astype(q_s.dtype)
    for j in range(KV_WIDTH // LANES):
        kb_s[WINDOW:, j * LANES:(j + 1) * LANES] = rope(
            proj_s[:, OFF_K + j * LANES:OFF_K + (j + 1) * LANES]).astype(kb_s.dtype)
    vb_s[WINDOW:, :] = proj_s[:, OFF_V:OFF_V + KV_WIDTH].astype(vb_s.dtype)

    row = lax.broadcasted_iota(jnp.int32, (WINDOW, 2 * WINDOW), 0)
    col = lax.broadcasted_iota(jnp.int32, (WINDOW, 2 * WINDOW), 1)
    band = (col > row) & (col <= row + WINDOW)
    group = N_HEADS // N_KV_HEADS
    for i in range(tb // WINDOW):
        has_prev = jnp.logical_or(t_idx > 0, i > 0)
        mask = band & jnp.logical_or(col >= WINDOW, has_prev)
        outs = []
        for head in range(N_HEADS):
            kv = head // group
            qh = q_s[i * WINDOW:(i + 1) * WINDOW, head * HEAD_DIM:(head + 1) * HEAD_DIM]
            kh = kb_s[i * WINDOW:(i + 2) * WINDOW, kv * HEAD_DIM:(kv + 1) * HEAD_DIM]
            vh = vb_s[i * WINDOW:(i + 2) * WINDOW, kv * HEAD_DIM:(kv + 1) * HEAD_DIM]
            s = jnp.where(mask, _dot_nt(qh, kh), MASK_VALUE)
            sink = sinks_ref[head]
            m = jnp.maximum(jnp.max(s, axis=-1, keepdims=True), sink)
            p = jnp.exp(s - m)
            denom = jnp.sum(p, axis=-1, keepdims=True) + jnp.exp(sink - m)
            outs.append(_dot_nn(p.astype(BF16), vh) / denom)
        attn = jnp.concatenate(outs, axis=1)
        z = proj_s[i * WINDOW:(i + 1) * WINDOW, OFF_Z:OFF_Z + ATTN_WIDTH]
        a_s[i * WINDOW:(i + 1) * WINDOW, :] = (attn * (z * _sigmoid(z))).astype(a_s.dtype)
    kb_s[0:WINDOW, :] = kb_s[tb:tb + WINDOW, :]
    vb_s[0:WINDOW, :] = vb_s[tb:tb + WINDOW, :]

    lbl = lbl_ref[...]
    lb_e = jnp.exp(lbl - jnp.max(lbl, axis=0, keepdims=True))
    lb_soft = lb_e / jnp.sum(lb_e, axis=0, keepdims=True)
    lb = jnp.sum(lb_soft[0:layer + 1, :], axis=0, keepdims=True) - lb_soft[0:1, :]
    one_minus_lb = 1.0 - lb
    hnw = hnw_ref[...]

    n_lvl = len(LEVELS)
    pair_width = 2 * HGRN_HEAD_DIM
    for c in range(tb // CHUNK):
        rows = slice(c * CHUNK, (c + 1) * CHUNK)
        fx = proj_s[rows, OFF_HF:OFF_HF + HGRN_WIDTH]
        e = jnp.exp(-jnp.abs(fx))
        r = 1.0 / (1.0 + e)
        er = e * r
        pos_f = fx >= 0
        logf = jnp.log(lb + one_minus_lb * jnp.where(pos_f, r, er))
        kin = one_minus_lb * jnp.where(pos_f, er, r)
        qx = proj_s[rows, OFF_HQ:OFF_HQ + HGRN_WIDTH]
        qin = qx * _sigmoid(qx)
        vin = proj_s[rows, OFF_HI:OFF_HI + HGRN_WIDTH]
        gx = proj_s[rows, OFF_HG:OFF_HG + HGRN_WIDTH]
        gate = gx * _sigmoid(gx)

        decay = jnp.exp(_dot_nn(sums_ref[...], jnp.concatenate(_split3(logf), axis=0)))

        for pair in range(HGRN_HEADS // 2):
            ps = slice(pair * pair_width, (pair + 1) * pair_width)
            q2 = qin[:, ps]
            k2 = kin[:, ps]
            v2 = vin[:, ps].astype(BF16)
            scores = masks_ref[n_lvl] * _dot_nt(q2.astype(BF16), _block_diag2(k2.astype(BF16)))
            for li in range(n_lvl):
                d_l = decay[li * CHUNK:(li + 1) * CHUNK, ps]
                scores = scores + masks_ref[li] * _dot_nt(
                    (q2 * d_l).astype(BF16), _block_diag2((k2 * d_l).astype(BF16)))
            o2 = _dot_nn(scores.astype(BF16), _block_diag2(v2))
            g_cum = decay[n_lvl * CHUNK:(n_lvl + 1) * CHUNK, ps]
            g_rest = decay[(n_lvl + 1) * CHUNK:(n_lvl + 2) * CHUNK, ps]
            qg = (q2 * g_cum).astype(BF16)
            kg = (k2 * g_rest).astype(BF16)
            for j in range(2):
                head = 2 * pair + j
                ls = slice(j * HGRN_HEAD_DIM, (j + 1) * HGRN_HEAD_DIM)
                hs = slice(head * HGRN_HEAD_DIM, (head + 1) * HGRN_HEAD_DIM)
                state = st_s[head]
                o = o2[:, ls] + _dot_nt(qg[:, ls], state.astype(BF16))
                st_s[head] = state * g_cum[CHUNK - 1:CHUNK, ls] + _dot_tn(v2[:, ls], kg[:, ls])
                o = o * lax.rsqrt(jnp.mean(o * o, axis=-1, keepdims=True) + NORM_EPS) * hnw
                b_s[rows, hs] = (o * gate[:, hs]).astype(b_s.dtype)

    up_a = _dot_nn(a_s[...], wua_ref[...])
    up_h = _dot_nn(b_s[...], wuh_ref[...])
    merged = (_sigmoid(proj_s[:, OFF_GA:OFF_GA + D_MODEL]) * up_a
              + _sigmoid(proj_s[:, OFF_GH:OFF_GH + D_MODEL]) * up_h)
    y = x + _dot_nn(merged.astype(BF16), wo_ref[...])
    if final:
        y = _rms_norm(y, fnw_ref[...])
    o_ref[0] = y


def _layer_call(x, cos_tab, sin_tab, sinks, norm_w, w_in, hgrn_norm_w, w_up_attn, w_up_hgrn, w_out,
                lb_logits, sums, masks, final_norm_w, *, layer, final):
    batch, seq, d = x.shape
    tb = TOKEN_BLOCK
    const2 = lambda b, t: (0, 0)
    const3 = lambda b, t: (0, 0, 0)
    tok = lambda b, t: (b, t, 0)
    return pl.pallas_call(
        functools.partial(_layer_kernel, layer=layer, final=final),
        grid=(batch, seq // tb),
        in_specs=[
            pl.BlockSpec(memory_space=pltpu.SMEM),
            pl.BlockSpec((1, tb, d), tok),
            pl.BlockSpec((1, tb, LANES), tok),
            pl.BlockSpec((1, tb, LANES), tok),
            pl.BlockSpec((1, d), const2),
            pl.BlockSpec((d, IN_WIDTH), const2),
            pl.BlockSpec((1, HGRN_HEAD_DIM), const2),
            pl.BlockSpec((ATTN_WIDTH, d), const2),
            pl.BlockSpec((HGRN_WIDTH, d), const2),
            pl.BlockSpec((d, d), const2),
            pl.BlockSpec((DEPTH, HGRN_WIDTH), const2),
            pl.BlockSpec((N_SUMS * CHUNK, 3 * CHUNK), const2),
            pl.BlockSpec((len(LEVELS) + 1, CHUNK, 2 * CHUNK), const3),
            pl.BlockSpec((1, d), const2),
        ],
        out_specs=pl.BlockSpec((1, tb, d), tok),
        out_shape=jax.ShapeDtypeStruct(x.shape, x.dtype),
        scratch_shapes=[
            pltpu.VMEM((tb, IN_WIDTH), F32),
            pltpu.VMEM((tb, ATTN_WIDTH), BF16),
            pltpu.VMEM((WINDOW + tb, KV_WIDTH), BF16),
            pltpu.VMEM((WINDOW + tb, KV_WIDTH), BF16),
            pltpu.VMEM((HGRN_HEADS, HGRN_HEAD_DIM, HGRN_HEAD_DIM), F32),
            pltpu.VMEM((tb, ATTN_WIDTH), BF16),
            pltpu.VMEM((tb, HGRN_WIDTH), BF16),
        ],
        compiler_params=pltpu.CompilerParams(
            dimension_semantics=("arbitrary", "arbitrary"),
            vmem_limit_bytes=VMEM_LIMIT_BYTES),
        name=f"hybrid_layer_{layer}",
    )(sinks, x, cos_tab, sin_tab, norm_w, w_in, hgrn_norm_w, w_up_attn, w_up_hgrn, w_out,
      lb_logits, sums, masks, final_norm_w)


def kernel(x, positions, norm_w, w_in, attn_sinks, hgrn_norm_w, w_up_attn, w_up_hgrn, w_out, lb_logits,
           final_norm_w):
    depth = w_in.shape[0]
    assert depth == DEPTH and x.shape[1] % TOKEN_BLOCK == 0 and x.shape[1] % ROPE_BLOCK == 0
    cos_tab, sin_tab = _rope_tables(positions)
    sums_np, masks_np = _hgrn_constants()
    sums = jnp.asarray(sums_np, dtype=BF16)
    masks = jnp.asarray(masks_np, dtype=F32)
    fnw = final_norm_w.reshape(1, D_MODEL)
    for layer in range(depth):
        x = _layer_call(
            x, cos_tab, sin_tab, attn_sinks[layer], norm_w[layer].reshape(1, D_MODEL),
            w_in[layer].astype(BF16), hgrn_norm_w[layer].reshape(1, HGRN_HEAD_DIM),
            w_up_attn[layer].astype(BF16), w_up_hgrn[layer].astype(BF16), w_out[layer].astype(BF16),
            lb_logits, sums, masks, fnw, layer=layer, final=(layer == depth - 1))
    return x
```

```python
import functools

import numpy as np
import jax
import jax.numpy as jnp
from jax import lax
from jax.experimental import pallas as pl
from jax.experimental.pallas import tpu as pltpu

D_MODEL = 1024
DEPTH = 2
N_HEADS = 8
N_KV_HEADS = 2
HEAD_DIM = 64
ATTN_WIDTH = N_HEADS * HEAD_DIM
KV_WIDTH = N_KV_HEADS * HEAD_DIM
WINDOW = 128
ROT_DIM = HEAD_DIM // 4
ROPE_THETA = 500000.0
HGRN_HEADS = 4
HGRN_HEAD_DIM = 128
HGRN_WIDTH = HGRN_HEADS * HGRN_HEAD_DIM
CHUNK = 64
NORM_EPS = 1e-6
MASK_VALUE = -1e30
LOG2_E = 1.4426950408889634
IN_WIDTH = 2 * ATTN_WIDTH + 2 * KV_WIDTH + 4 * HGRN_WIDTH + 2 * D_MODEL

OFF_Q = 0
OFF_K = OFF_Q + ATTN_WIDTH
OFF_V = OFF_K + KV_WIDTH
OFF_Z = OFF_V + KV_WIDTH
OFF_HQ = OFF_Z + ATTN_WIDTH
OFF_HF = OFF_HQ + HGRN_WIDTH
OFF_HI = OFF_HF + HGRN_WIDTH
OFF_HG = OFF_HI + HGRN_WIDTH
OFF_GA = OFF_HG + HGRN_WIDTH
OFF_GH = OFF_GA + D_MODEL

LANES = 128
TOKEN_BLOCK = 256
ROPE_BLOCK = 512
PROJ_PIECE = 256
VMEM_LIMIT_BYTES = 56 * 1024 * 1024

LEVELS = (32, 16, 8, 4, 2, 1)
N_SUMS = len(LEVELS) + 2

F32 = jnp.float32
BF16 = jnp.bfloat16


def _dot_nn(a, b):
    return lax.dot_general(a, b, (((1,), (0,)), ((), ())), preferred_element_type=F32)


def _dot_nt(a, b):
    return lax.dot_general(a, b, (((1,), (1,)), ((), ())), preferred_element_type=F32)


def _dot_tn(a, b):
    return lax.dot_general(a, b, (((0,), (0,)), ((), ())), preferred_element_type=F32)


def _split3(x):
    hi = x.astype(BF16)
    r1 = x - hi.astype(F32)
    mid = r1.astype(BF16)
    lo = (r1 - mid.astype(F32)).astype(BF16)
    return hi, mid, lo


def _sigmoid(x):
    return 0.5 * jnp.tanh(0.5 * x) + 0.5


def _block_diag2(a):
    zero = jnp.zeros((a.shape[0], LANES), a.dtype)
    top = jnp.concatenate([a[:, :LANES], zero], axis=1)
    bottom = jnp.concatenate([zero, a[:, LANES:]], axis=1)
    return jnp.concatenate([top, bottom], axis=0)


def _hgrn_constants():
    t = np.arange(CHUNK)[:, None]
    u = np.arange(CHUNK)[None, :]
    sums, masks = [], []
    for h in LEVELS:
        right = (t // h) % 2 == 1
        m = (t // (2 * h)) * 2 * h + h - 1
        sums.append(np.where(right, (u > m) & (u <= t), (u > t) & (u <= m)))
        masks.append((t // (2 * h) == u // (2 * h)) & right & ((u // h) % 2 == 0))
    sums.append(u <= t)
    sums.append(u > t)
    masks.append(t == u)
    sums = np.concatenate(sums, axis=0).astype(np.float32)
    masks = np.stack(masks, axis=0).astype(np.float32)
    return np.concatenate([sums] * 3, axis=1), np.concatenate([masks] * 2, axis=2)


def _rope_expand_matrix():
    half = ROT_DIM // 2
    e = np.zeros((2 * half, 2 * LANES), np.float32)
    for lane in range(LANES):
        d = lane % HEAD_DIM
        if d < half:
            e[d, lane] = 1.0
            e[half + d, LANES + lane] = -1.0
        elif d < ROT_DIM:
            e[d - half, lane] = 1.0
            e[half + d - half, LANES + lane] = 1.0
    return e


def _rope_table_kernel(pos_ref, invf_ref, expand_ref, cos_ref, sin_ref):
    batch = pos_ref.shape[0]
    lane = lax.broadcasted_iota(jnp.int32, (1, LANES), 1) % HEAD_DIM
    passthrough = jnp.where(lane >= ROT_DIM, 1.0, 0.0).astype(F32)
    expand = expand_ref[...]
    for b in range(batch):
        pos = pos_ref[b:b + 1, :].astype(F32)
        ang = invf_ref[...] * pos
        cs = jnp.concatenate([jnp.cos(ang), jnp.sin(ang)], axis=0)
        out = None
        for piece in _split3(cs):
            part = _dot_tn(piece, expand)
            out = part if out is None else out + part
        cos_ref[b] = out[:, :LANES] + passthrough
        sin_ref[b] = out[:, LANES:]


def _rope_tables(positions):
    batch, seq = positions.shape
    half = ROT_DIM // 2
    inv_freq = jnp.power(ROPE_THETA, -jnp.arange(half, dtype=F32) * (2.0 / ROT_DIM)).reshape(half, 1)
    expand = jnp.asarray(_rope_expand_matrix(), dtype=BF16)
    out_sds = jax.ShapeDtypeStruct((batch, seq, LANES), F32)
    return pl.pallas_call(
        _rope_table_kernel,
        grid=(seq // ROPE_BLOCK,),
        in_specs=[
            pl.BlockSpec((batch, ROPE_BLOCK), lambda t: (0, t)),
            pl.BlockSpec((half, 1), lambda t: (0, 0)),
            pl.BlockSpec((2 * half, 2 * LANES), lambda t: (0, 0)),
        ],
        out_specs=[
            pl.BlockSpec((batch, ROPE_BLOCK, LANES), lambda t: (0, t, 0)),
            pl.BlockSpec((batch, ROPE_BLOCK, LANES), lambda t: (0, t, 0)),
        ],
        out_shape=[out_sds, out_sds],
        name="rope_tables",
    )(positions, inv_freq, expand)


def _rms_norm(x, w):
    return x * lax.rsqrt(jnp.mean(x * x, axis=-1, keepdims=True) + NORM_EPS) * w


def _layer_kernel(sinks_ref, x_ref, cos_ref, sin_ref, nw_ref, win_ref, hnw_ref, wua_ref, wuh_ref,
                  wo_ref, lbl_ref, sums_ref, masks_ref, fnw_ref, o_ref,
                  proj0_s, proj1_s, x0_s, x1_s, h_s, q_s, kb_s, vb_s, st_s, a_s, b_s,
                  *, layer, final, blocks_per_seq):
    g = pl.program_id(0)
    t_idx = (g + blocks_per_seq - 1) % blocks_per_seq

    @pl.when(g == 0)
    def _():
        proj1_s[...] = jnp.zeros(proj1_s.shape, proj1_s.dtype)
        x1_s[...] = jnp.zeros(x1_s.shape, x1_s.dtype)

    @pl.when(jnp.logical_or(t_idx == 0, g == 0))
    def _():
        kb_s[0:WINDOW, :] = jnp.zeros((WINDOW, KV_WIDTH), kb_s.dtype)
        vb_s[0:WINDOW, :] = jnp.zeros((WINDOW, KV_WIDTH), vb_s.dtype)
        st_s[...] = jnp.zeros(st_s.shape, st_s.dtype)

    step = functools.partial(
        _layer_step, sinks_ref, x_ref, cos_ref, sin_ref, nw_ref, win_ref, hnw_ref, wua_ref, wuh_ref,
        wo_ref, lbl_ref, sums_ref, masks_ref, fnw_ref, o_ref, h_s, q_s, kb_s, vb_s, st_s, a_s, b_s,
        t_idx=t_idx, layer=layer, final=final)

    @pl.when(g % 2 == 0)
    def _():
        step(proj0_s, x0_s, proj1_s, x1_s)

    @pl.when(g % 2 == 1)
    def _():
        step(proj1_s, x1_s, proj0_s, x0_s)


def _layer_step(sinks_ref, x_ref, cos_ref, sin_ref, nw_ref, win_ref, hnw_ref, wua_ref, wuh_ref,
                wo_ref, lbl_ref, sums_ref, masks_ref, fnw_ref, o_ref, h_s, q_s, kb_s, vb_s, st_s, a_s, b_s,
                proj_in_s, x_in_s, proj_s, x_s, *, t_idx, layer, final):
    tb = x_ref.shape[1]

    x_in = x_ref[0]
    x_in_s[...] = x_in
    h_s[...] = _rms_norm(x_in, nw_ref[...]).astype(BF16)
    pieces = iter(range(0, IN_WIDTH, PROJ_PIECE))

    def project(n_pieces):
        for _ in range(n_pieces):
            c0 = next(pieces, None)
            if c0 is not None:
                proj_in_s[:, c0:c0 + PROJ_PIECE] = _dot_nn(h_s[...], win_ref[:, c0:c0 + PROJ_PIECE])

    x = x_s[...]

    half = ROT_DIM // 2
    lane = lax.broadcasted_iota(jnp.int32, (1, LANES), 1) % HEAD_DIM
    first_half = lane < half
    cos_t = cos_ref[0]
    sin_t = sin_ref[0]

    def rope(tile):
        partner = jnp.where(first_half, pltpu.roll(tile, LANES - half, axis=1), pltpu.roll(tile, half, axis=1))
        return tile * cos_t + partner * sin_t

    scale = HEAD_DIM ** -0.5 * LOG2_E
    for j in range(ATTN_WIDTH // LANES):
        q_s[:, j * LANES:(j + 1) * LANES] = (
            rope(proj_s[:, OFF_Q + j * LANES:OFF_Q + (j + 1) * LANES]) * scale).astype(q_s.dtype)
    for j in range(KV_WIDTH // LANES):
        kb_s[WINDOW:, j * LANES:(j + 1) * LANES] = rope(
            proj_s[:, OFF_K + j * LANES:OFF_K + (j + 1) * LANES]).astype(kb_s.dtype)
    vb_s[WINDOW:, :] = proj_s[:, OFF_V:OFF_V + KV_WIDTH].astype(vb_s.dtype)

    row = lax.broadcasted_iota(jnp.int32, (2 * WINDOW, 2 * WINDOW), 0) % WINDOW
    col = lax.broadcasted_iota(jnp.int32, (2 * WINDOW, 2 * WINDOW), 1)
    band = (col > row) & (col <= row + WINDOW)
    upper_rows = lax.broadcasted_iota(jnp.int32, (2 * WINDOW, 1), 0) < WINDOW
    group = N_HEADS // N_KV_HEADS
    pairs_per_block = N_HEADS // 2
    n_att = (tb // WINDOW) * pairs_per_block
    att = [dict() for _ in range(n_att)]

    def att_scores(k):
        i, pr = divmod(k, pairs_per_block)
        heads = (2 * pr, 2 * pr + 1)
        kv = heads[0] // group
        q2 = jnp.concatenate(
            [q_s[i * WINDOW:(i + 1) * WINDOW, h * HEAD_DIM:(h + 1) * HEAD_DIM] for h in heads], axis=0)
        kh = kb_s[i * WINDOW:(i + 2) * WINDOW, kv * HEAD_DIM:(kv + 1) * HEAD_DIM]
        has_prev = jnp.logical_or(t_idx > 0, i > 0)
        mask = band & jnp.logical_or(col >= WINDOW, has_prev)
        att[k]["s"] = jnp.where(mask, _dot_nt(q2, kh), MASK_VALUE)
        att[k]["sink"] = jnp.where(upper_rows, sinks_ref[heads[0]], sinks_ref[heads[1]]) * LOG2_E

    def att_max(k):
        att[k]["m"] = jnp.maximum(jnp.max(att[k]["s"], axis=-1, keepdims=True), att[k]["sink"])

    def att_probs(k):
        m = att[k]["m"]
        p = jnp.exp2(att[k].pop("s") - m)
        att[k]["rdenom"] = 1.0 / (jnp.sum(p, axis=-1, keepdims=True) + jnp.exp2(att[k].pop("sink") - m))
        att[k]["p"] = p.astype(BF16)

    def att_out(k):
        i, pr = divmod(k, pairs_per_block)
        kv = (2 * pr) // group
        vh = vb_s[i * WINDOW:(i + 2) * WINDOW, kv * HEAD_DIM:(kv + 1) * HEAD_DIM]
        o = _dot_nn(att[k].pop("p"), vh) * att[k].pop("rdenom")
        o = jnp.concatenate([o[:WINDOW], o[WINDOW:]], axis=1)
        z = proj_s[i * WINDOW:(i + 1) * WINDOW, OFF_Z + pr * LANES:OFF_Z + (pr + 1) * LANES]
        a_s[i * WINDOW:(i + 1) * WINDOW, pr * LANES:(pr + 1) * LANES] = (o * (z * _sigmoid(z))).astype(a_s.dtype)

    att_stages = (att_scores, att_max, att_probs, att_out)

    lbl = lbl_ref[...]
    lb_e = jnp.exp(lbl - jnp.max(lbl, axis=0, keepdims=True))
    lb_soft = lb_e / jnp.sum(lb_e, axis=0, keepdims=True)
    lb = jnp.sum(lb_soft[0:layer + 1, :], axis=0, keepdims=True) - lb_soft[0:1, :]
    one_minus_lb = 1.0 - lb
    hnw = hnw_ref[...]

    n_lvl = len(LEVELS)
    pair_width = 2 * HGRN_HEAD_DIM
    n_chunks = tb // CHUNK
    n_pairs = HGRN_HEADS // 2
    hg = [dict() for _ in range(n_chunks)]

    def hgrn_gates(c):
        rows = slice(c * CHUNK, (c + 1) * CHUNK)
        fx = proj_s[rows, OFF_HF:OFF_HF + HGRN_WIDTH]
        e = jnp.exp(-jnp.abs(fx))
        r = 1.0 / (1.0 + e)
        er = e * r
        pos_f = fx >= 0
        logf = jnp.log(lb + one_minus_lb * jnp.where(pos_f, r, er))
        hg[c]["k"] = one_minus_lb * jnp.where(pos_f, er, r)
        qx = proj_s[rows, OFF_HQ:OFF_HQ + HGRN_WIDTH]
        hg[c]["q"] = qx * _sigmoid(qx)
        hg[c]["logf3"] = jnp.concatenate(_split3(logf), axis=0)

    def hgrn_decay(c):
        hg[c]["decay"] = jnp.exp(_dot_nn(sums_ref[...], hg[c].pop("logf3")))

    def hgrn_scores(c):
        decay = hg[c]["decay"]
        hg[c]["scores"] = []
        for pair in range(n_pairs):
            ps = slice(pair * pair_width, (pair + 1) * pair_width)
            q2 = hg[c]["q"][:, ps]
            k2 = hg[c]["k"][:, ps]
            scores = masks_ref[n_lvl] * _dot_nt(q2.astype(BF16), _block_diag2(k2.astype(BF16)))
            for li in range(n_lvl):
                d_l = decay[li * CHUNK:(li + 1) * CHUNK, ps]
                scores = scores + masks_ref[li] * _dot_nt(
                    (q2 * d_l).astype(BF16), _block_diag2((k2 * d_l).astype(BF16)))
            hg[c]["scores"].append(scores.astype(BF16))

    def hgrn_out(c):
        rows = slice(c * CHUNK, (c + 1) * CHUNK)
        decay = hg[c].pop("decay")
        qin = hg[c].pop("q")
        kin = hg[c].pop("k")
        gx = proj_s[rows, OFF_HG:OFF_HG + HGRN_WIDTH]
        gate = gx * _sigmoid(gx)
        for pair in range(n_pairs):
            ps = slice(pair * pair_width, (pair + 1) * pair_width)
            v2 = proj_s[rows, OFF_HI + pair * pair_width:OFF_HI + (pair + 1) * pair_width].astype(BF16)
            o2 = _dot_nn(hg[c]["scores"][pair], _block_diag2(v2))
            g_cum = decay[n_lvl * CHUNK:(n_lvl + 1) * CHUNK, ps]
            g_rest = decay[(n_lvl + 1) * CHUNK:(n_lvl + 2) * CHUNK, ps]
            qg = (qin[:, ps] * g_cum).astype(BF16)
            kg = (kin[:, ps] * g_rest).astype(BF16)
            for j in range(2):
                head = 2 * pair + j
                ls = slice(j * HGRN_HEAD_DIM, (j + 1) * HGRN_HEAD_DIM)
                hs = slice(head * HGRN_HEAD_DIM, (head + 1) * HGRN_HEAD_DIM)
                state = st_s[head]
                o = o2[:, ls] + _dot_nt(qg[:, ls], state.astype(BF16))
                st_s[head] = state * g_cum[CHUNK - 1:CHUNK, ls] + _dot_tn(v2[:, ls], kg[:, ls])
                o = o * lax.rsqrt(jnp.mean(o * o, axis=-1, keepdims=True) + NORM_EPS) * hnw
                b_s[rows, hs] = (o * gate[:, hs]).astype(b_s.dtype)
        hg[c].clear()

    hgrn_stages = (hgrn_gates, hgrn_decay, hgrn_scores, hgrn_out)

    n_slots = max(n_att + len(att_stages) - 1, 2 * (n_chunks - 1) + len(hgrn_stages))
    pieces_per_slot = -(-(IN_WIDTH // PROJ_PIECE) // n_slots)
    for slot in range(n_slots):
        for s, stage in enumerate(att_stages):
            if 0 <= slot - s < n_att:
                stage(slot - s)
        for s, stage in enumerate(hgrn_stages):
            if (slot - s) % 2 == 0 and 0 <= (slot - s) // 2 < n_chunks:
                stage((slot - s) // 2)
        project(pieces_per_slot)
    kb_s[0:WINDOW, :] = kb_s[tb:tb + WINDOW, :]
    vb_s[0:WINDOW, :] = vb_s[tb:tb + WINDOW, :]

    project(IN_WIDTH // PROJ_PIECE)
    up_a = _dot_nn(a_s[...], wua_ref[...])
    up_h = _dot_nn(b_s[...], wuh_ref[...])
    merged = (_sigmoid(proj_s[:, OFF_GA:OFF_GA + D_MODEL]) * up_a
              + _sigmoid(proj_s[:, OFF_GH:OFF_GH + D_MODEL]) * up_h)
    y = x + _dot_nn(merged.astype(BF16), wo_ref[...])
    if final:
        y = _rms_norm(y, fnw_ref[...])
    o_ref[0] = y


def _layer_call(x, cos_tab, sin_tab, sinks, norm_w, w_in, hgrn_norm_w, w_up_attn, w_up_hgrn, w_out,
                lb_logits, sums, masks, final_norm_w, *, layer, final):
    batch, seq, d = x.shape
    tb = TOKEN_BLOCK
    n_blocks = batch * seq // tb
    once = pl.Buffered(1)
    const2 = lambda g: (0, 0)
    const3 = lambda g: (0, 0, 0)
    blk_in = lambda g: (jnp.minimum(g, n_blocks - 1), 0, 0)
    blk_mix = lambda g: (jnp.maximum(g - 1, 0), 0, 0)
    out = pl.pallas_call(
        functools.partial(_layer_kernel, layer=layer, final=final, blocks_per_seq=seq // tb),
        grid=(n_blocks + 1,),
        in_specs=[
            pl.BlockSpec(memory_space=pltpu.SMEM),
            pl.BlockSpec((1, tb, d), blk_in),
            pl.BlockSpec((1, tb, LANES), blk_mix),
            pl.BlockSpec((1, tb, LANES), blk_mix),
            pl.BlockSpec((1, d), const2, pipeline_mode=once),
            pl.BlockSpec((d, IN_WIDTH), const2, pipeline_mode=once),
            pl.BlockSpec((1, HGRN_HEAD_DIM), const2, pipeline_mode=once),
            pl.BlockSpec((ATTN_WIDTH, d), const2, pipeline_mode=once),
            pl.BlockSpec((HGRN_WIDTH, d), const2, pipeline_mode=once),
            pl.BlockSpec((d, d), const2, pipeline_mode=once),
            pl.BlockSpec((DEPTH, HGRN_WIDTH), const2, pipeline_mode=once),
            pl.BlockSpec((N_SUMS * CHUNK, 3 * CHUNK), const2, pipeline_mode=once),
            pl.BlockSpec((len(LEVELS) + 1, CHUNK, 2 * CHUNK), const3, pipeline_mode=once),
            pl.BlockSpec((1, d), const2, pipeline_mode=once),
        ],
        out_specs=pl.BlockSpec((1, tb, d), blk_mix),
        out_shape=jax.ShapeDtypeStruct((n_blocks, tb, d), x.dtype),
        scratch_shapes=[
            pltpu.VMEM((tb, IN_WIDTH), F32),
            pltpu.VMEM((tb, IN_WIDTH), F32),
            pltpu.VMEM((tb, d), F32),
            pltpu.VMEM((tb, d), F32),
            pltpu.VMEM((tb, d), BF16),
            pltpu.VMEM((tb, ATTN_WIDTH), BF16),
            pltpu.VMEM((WINDOW + tb, KV_WIDTH), BF16),
            pltpu.VMEM((WINDOW + tb, KV_WIDTH), BF16),
            pltpu.VMEM((HGRN_HEADS, HGRN_HEAD_DIM, HGRN_HEAD_DIM), F32),
            pltpu.VMEM((tb, ATTN_WIDTH), BF16),
            pltpu.VMEM((tb, HGRN_WIDTH), BF16),
        ],
        compiler_params=pltpu.CompilerParams(
            dimension_semantics=("arbitrary",),
            vmem_limit_bytes=VMEM_LIMIT_BYTES),
        name=f"hybrid_layer_{layer}",
    )(sinks, x.reshape(n_blocks, tb, d), cos_tab.reshape(n_blocks, tb, LANES),
      sin_tab.reshape(n_blocks, tb, LANES), norm_w, w_in, hgrn_norm_w, w_up_attn, w_up_hgrn, w_out,
      lb_logits, sums, masks, final_norm_w)
    return out.reshape(batch, seq, d)


def kernel(x, positions, norm_w, w_in, attn_sinks, hgrn_norm_w, w_up_attn, w_up_hgrn, w_out, lb_logits,
           final_norm_w):
    depth = w_in.shape[0]
    assert depth == DEPTH and x.shape[1] % TOKEN_BLOCK == 0 and x.shape[1] % ROPE_BLOCK == 0
    cos_tab, sin_tab = _rope_tables(positions)
    sums_np, masks_np = _hgrn_constants()
    sums = jnp.asarray(sums_np, dtype=BF16)
    masks = jnp.asarray(masks_np, dtype=F32)
    fnw = final_norm_w.reshape(1, D_MODEL)
    for layer in range(depth):
        x = _layer_call(
            x, cos_tab, sin_tab, attn_sinks[layer], norm_w[layer].reshape(1, D_MODEL),
            w_in[layer].astype(BF16), hgrn_norm_w[layer].reshape(1, HGRN_HEAD_DIM),
            w_up_attn[layer].astype(BF16), w_up_hgrn[layer].astype(BF16), w_out[layer].astype(BF16),
            lb_logits, sums, masks, fnw, layer=layer, final=(layer == depth - 1))
    return x
```

```python
import functools

import numpy as np
import jax
import jax.numpy as jnp
from jax import lax
from jax.experimental import pallas as pl
from jax.experimental.pallas import tpu as pltpu

D_MODEL = 1024
DEPTH = 2
N_HEADS = 8
N_KV_HEADS = 2
HEAD_DIM = 64
ATTN_WIDTH = N_HEADS * HEAD_DIM
KV_WIDTH = N_KV_HEADS * HEAD_DIM
WINDOW = 128
ROT_DIM = HEAD_DIM // 4
ROPE_THETA = 500000.0
HGRN_HEADS = 4
HGRN_HEAD_DIM = 128
HGRN_WIDTH = HGRN_HEADS * HGRN_HEAD_DIM
CHUNK = 64
NORM_EPS = 1e-6
MASK_VALUE = -1e30
LOG2_E = 1.4426950408889634
IN_WIDTH = 2 * ATTN_WIDTH + 2 * KV_WIDTH + 4 * HGRN_WIDTH + 2 * D_MODEL

OFF_Q = 0
OFF_K = OFF_Q + ATTN_WIDTH
OFF_V = OFF_K + KV_WIDTH
OFF_Z = OFF_V + KV_WIDTH
OFF_HQ = OFF_Z + ATTN_WIDTH
OFF_HF = OFF_HQ + HGRN_WIDTH
OFF_HI = OFF_HF + HGRN_WIDTH
OFF_HG = OFF_HI + HGRN_WIDTH
OFF_GA = OFF_HG + HGRN_WIDTH
OFF_GH = OFF_GA + D_MODEL

LANES = 128
TOKEN_BLOCK = 256
ROPE_BLOCK = 512
PROJ_PIECE = 256
TAIL_PIECES = 5
VMEM_LIMIT_BYTES = 56 * 1024 * 1024

LEVELS = (32, 16, 8, 4, 2, 1)
N_SUMS = len(LEVELS) + 2

F32 = jnp.float32
BF16 = jnp.bfloat16


def _dot_nn(a, b):
    return lax.dot_general(a, b, (((1,), (0,)), ((), ())), preferred_element_type=F32)


def _dot_nt(a, b):
    return lax.dot_general(a, b, (((1,), (1,)), ((), ())), preferred_element_type=F32)


def _dot_tn(a, b):
    return lax.dot_general(a, b, (((0,), (0,)), ((), ())), preferred_element_type=F32)


def _split3(x):
    hi = x.astype(BF16)
    r1 = x - hi.astype(F32)
    mid = r1.astype(BF16)
    lo = (r1 - mid.astype(F32)).astype(BF16)
    return hi, mid, lo


def _sigmoid(x):
    return 0.5 * jnp.tanh(0.5 * x) + 0.5


def _block_diag2(a):
    zero = jnp.zeros((a.shape[0], LANES), a.dtype)
    top = jnp.concatenate([a[:, :LANES], zero], axis=1)
    bottom = jnp.concatenate([zero, a[:, LANES:]], axis=1)
    return jnp.concatenate([top, bottom], axis=0)


def _hgrn_constants():
    t = np.arange(CHUNK)[:, None]
    u = np.arange(CHUNK)[None, :]
    sums, masks = [], []
    for h in LEVELS:
        right = (t // h) % 2 == 1
        m = (t // (2 * h)) * 2 * h + h - 1
        sums.append(np.where(right, (u > m) & (u <= t), (u > t) & (u <= m)))
        masks.append((t // (2 * h) == u // (2 * h)) & right & ((u // h) % 2 == 0))
    sums.append(u <= t)
    sums.append(u > t)
    masks.append(t == u)
    sums = np.concatenate(sums, axis=0).astype(np.float32)
    masks = np.stack(masks, axis=0).astype(np.float32)
    return np.concatenate([sums] * 3, axis=1), np.concatenate([masks] * 2, axis=2)


def _rope_expand_matrix():
    half = ROT_DIM // 2
    e = np.zeros((2 * half, 2 * LANES), np.float32)
    for lane in range(LANES):
        d = lane % HEAD_DIM
        if d < half:
            e[d, lane] = 1.0
            e[half + d, LANES + lane] = -1.0
        elif d < ROT_DIM:
            e[d - half, lane] = 1.0
            e[half + d - half, LANES + lane] = 1.0
    return e


def _rope_table_kernel(pos_ref, invf_ref, expand_ref, cos_ref, sin_ref):
    batch = pos_ref.shape[0]
    lane = lax.broadcasted_iota(jnp.int32, (1, LANES), 1) % HEAD_DIM
    passthrough = jnp.where(lane >= ROT_DIM, 1.0, 0.0).astype(F32)
    expand = expand_ref[...]
    for b in range(batch):
        pos = pos_ref[b:b + 1, :].astype(F32)
        ang = invf_ref[...] * pos
        cs = jnp.concatenate([jnp.cos(ang), jnp.sin(ang)], axis=0)
        out = None
        for piece in _split3(cs):
            part = _dot_tn(piece, expand)
            out = part if out is None else out + part
        cos_ref[b] = out[:, :LANES] + passthrough
        sin_ref[b] = out[:, LANES:]


def _rope_tables(positions):
    batch, seq = positions.shape
    half = ROT_DIM // 2
    inv_freq = jnp.power(ROPE_THETA, -jnp.arange(half, dtype=F32) * (2.0 / ROT_DIM)).reshape(half, 1)
    expand = jnp.asarray(_rope_expand_matrix(), dtype=BF16)
    out_sds = jax.ShapeDtypeStruct((batch, seq, LANES), F32)
    return pl.pallas_call(
        _rope_table_kernel,
        grid=(seq // ROPE_BLOCK,),
        in_specs=[
            pl.BlockSpec((batch, ROPE_BLOCK), lambda t: (0, t)),
            pl.BlockSpec((half, 1), lambda t: (0, 0)),
            pl.BlockSpec((2 * half, 2 * LANES), lambda t: (0, 0)),
        ],
        out_specs=[
            pl.BlockSpec((batch, ROPE_BLOCK, LANES), lambda t: (0, t, 0)),
            pl.BlockSpec((batch, ROPE_BLOCK, LANES), lambda t: (0, t, 0)),
        ],
        out_shape=[out_sds, out_sds],
        name="rope_tables",
    )(positions, inv_freq, expand)


def _rms_norm(x, w):
    return x * lax.rsqrt(jnp.mean(x * x, axis=-1, keepdims=True) + NORM_EPS) * w


def _layer_kernel(sinks_ref, xn_ref, xr_ref, cos_ref, sin_ref, nw_ref, win_ref, hnw_ref, wua_ref, wuh_ref,
                  wo_ref, lbl_ref, sums_ref, masks_ref, fnw_ref, o_ref,
                  proj0_s, proj1_s, h0_s, h1_s, q0_s, q1_s, kb0_s, kb1_s, vb0_s, vb1_s, st_s, a_s, b_s,
                  *, layer, final, blocks_per_seq):
    g = pl.program_id(0)
    t_idx = (g + blocks_per_seq - 1) % blocks_per_seq

    @pl.when(g == 0)
    def _():
        for ref in (proj1_s, q1_s, kb0_s, kb1_s, vb0_s, vb1_s):
            ref[...] = jnp.zeros(ref.shape, ref.dtype)
        h0_s[...] = _rms_norm(xr_ref[0], nw_ref[...]).astype(BF16)

    @pl.when(jnp.logical_or(t_idx == 0, g == 0))
    def _():
        st_s[...] = jnp.zeros(st_s.shape, st_s.dtype)

    step = functools.partial(
        _layer_step, sinks_ref, xn_ref, xr_ref, cos_ref, sin_ref, nw_ref, win_ref, hnw_ref, wua_ref,
        wuh_ref, wo_ref, lbl_ref, sums_ref, masks_ref, fnw_ref, o_ref, st_s, a_s, b_s,
        t_idx=t_idx, layer=layer, final=final)
    even = dict(proj=proj0_s, h=h0_s, q=q0_s, kb=kb0_s, vb=vb0_s)
    odd = dict(proj=proj1_s, h=h1_s, q=q1_s, kb=kb1_s, vb=vb1_s)

    @pl.when(g % 2 == 0)
    def _():
        step(even, odd)

    @pl.when(g % 2 == 1)
    def _():
        step(odd, even)


def _layer_step(sinks_ref, xn_ref, xr_ref, cos_ref, sin_ref, nw_ref, win_ref, hnw_ref, wua_ref, wuh_ref,
                wo_ref, lbl_ref, sums_ref, masks_ref, fnw_ref, o_ref, st_s, a_s, b_s, new, old,
                *, t_idx, layer, final):
    tb = xr_ref.shape[1]
    proj_s, q_s, kb_s, vb_s = old["proj"], old["q"], old["kb"], old["vb"]

    kb_s[0:WINDOW, :] = new["kb"][tb:tb + WINDOW, :]
    vb_s[0:WINDOW, :] = new["vb"][tb:tb + WINDOW, :]

    pieces = iter(range(0, IN_WIDTH, PROJ_PIECE))

    def project(n_pieces):
        for _ in range(n_pieces):
            c0 = next(pieces, None)
            if c0 is not None:
                new["proj"][:, c0:c0 + PROJ_PIECE] = _dot_nn(new["h"][...], win_ref[:, c0:c0 + PROJ_PIECE])

    def rotate_new():
        half = ROT_DIM // 2
        lane = lax.broadcasted_iota(jnp.int32, (1, LANES), 1) % HEAD_DIM
        first_half = lane < half
        cos_t = cos_ref[0]
        sin_t = sin_ref[0]

        def rope(tile):
            partner = jnp.where(first_half, pltpu.roll(tile, LANES - half, axis=1),
                                pltpu.roll(tile, half, axis=1))
            return tile * cos_t + partner * sin_t

        scale = HEAD_DIM ** -0.5 * LOG2_E
        for j in range(ATTN_WIDTH // LANES):
            new["q"][:, j * LANES:(j + 1) * LANES] = (
                rope(new["proj"][:, OFF_Q + j * LANES:OFF_Q + (j + 1) * LANES]) * scale).astype(BF16)
        for j in range(KV_WIDTH // LANES):
            new["kb"][WINDOW:, j * LANES:(j + 1) * LANES] = rope(
                new["proj"][:, OFF_K + j * LANES:OFF_K + (j + 1) * LANES]).astype(BF16)
        new["vb"][WINDOW:, :] = new["proj"][:, OFF_V:OFF_V + KV_WIDTH].astype(BF16)

    def normalise_next():
        old["h"][...] = _rms_norm(xn_ref[0], nw_ref[...]).astype(BF16)

    row = lax.broadcasted_iota(jnp.int32, (2 * WINDOW, 2 * WINDOW), 0) % WINDOW
    col = lax.broadcasted_iota(jnp.int32, (2 * WINDOW, 2 * WINDOW), 1)
    band = (col > row) & (col <= row + WINDOW)
    upper_rows = lax.broadcasted_iota(jnp.int32, (2 * WINDOW, 1), 0) < WINDOW
    group = N_HEADS // N_KV_HEADS
    pairs_per_block = N_HEADS // 2
    n_att = (tb // WINDOW) * pairs_per_block
    att = [dict() for _ in range(n_att)]

    def att_scores(k):
        i, pr = divmod(k, pairs_per_block)
        heads = (2 * pr, 2 * pr + 1)
        kv = heads[0] // group
        q2 = jnp.concatenate(
            [q_s[i * WINDOW:(i + 1) * WINDOW, h * HEAD_DIM:(h + 1) * HEAD_DIM] for h in heads], axis=0)
        kh = kb_s[i * WINDOW:(i + 2) * WINDOW, kv * HEAD_DIM:(kv + 1) * HEAD_DIM]
        has_prev = jnp.logical_or(t_idx > 0, i > 0)
        mask = band & jnp.logical_or(col >= WINDOW, has_prev)
        att[k]["s"] = jnp.where(mask, _dot_nt(q2, kh), MASK_VALUE)
        att[k]["sink"] = jnp.where(upper_rows, sinks_ref[heads[0]], sinks_ref[heads[1]]) * LOG2_E

    def att_max(k):
        att[k]["m"] = jnp.maximum(jnp.max(att[k]["s"], axis=-1, keepdims=True), att[k]["sink"])

    def att_probs(k):
        m = att[k]["m"]
        p = jnp.exp2(att[k].pop("s") - m)
        att[k]["rdenom"] = 1.0 / (jnp.sum(p, axis=-1, keepdims=True) + jnp.exp2(att[k].pop("sink") - m))
        att[k]["p"] = p.astype(BF16)

    def att_out(k):
        i, pr = divmod(k, pairs_per_block)
        kv = (2 * pr) // group
        vh = vb_s[i * WINDOW:(i + 2) * WINDOW, kv * HEAD_DIM:(kv + 1) * HEAD_DIM]
        o = _dot_nn(att[k].pop("p"), vh) * att[k].pop("rdenom")
        o = jnp.concatenate([o[:WINDOW], o[WINDOW:]], axis=1)
        z = proj_s[i * WINDOW:(i + 1) * WINDOW, OFF_Z + pr * LANES:OFF_Z + (pr + 1) * LANES]
        a_s[i * WINDOW:(i + 1) * WINDOW, pr * LANES:(pr + 1) * LANES] = (o * (z * _sigmoid(z))).astype(a_s.dtype)

    att_stages = (att_scores, att_max, att_probs, att_out)

    lbl = lbl_ref[...]
    lb_e = jnp.exp(lbl - jnp.max(lbl, axis=0, keepdims=True))
    lb_soft = lb_e / jnp.sum(lb_e, axis=0, keepdims=True)
    lb = jnp.sum(lb_soft[0:layer + 1, :], axis=0, keepdims=True) - lb_soft[0:1, :]
    one_minus_lb = 1.0 - lb
    hnw = hnw_ref[...]

    n_lvl = len(LEVELS)
    pair_width = 2 * HGRN_HEAD_DIM
    n_chunks = tb // CHUNK
    n_pairs = HGRN_HEADS // 2
    hg = [dict() for _ in range(n_chunks)]

    def hgrn_gates(c):
        rows = slice(c * CHUNK, (c + 1) * CHUNK)
        fx = proj_s[rows, OFF_HF:OFF_HF + HGRN_WIDTH]
        e = jnp.exp(-jnp.abs(fx))
        r = 1.0 / (1.0 + e)
        er = e * r
        pos_f = fx >= 0
        logf = jnp.log(lb + one_minus_lb * jnp.where(pos_f, r, er))
        hg[c]["k"] = one_minus_lb * jnp.where(pos_f, er, r)
        qx = proj_s[rows, OFF_HQ:OFF_HQ + HGRN_WIDTH]
        hg[c]["q"] = qx * _sigmoid(qx)
        hg[c]["logf3"] = jnp.concatenate(_split3(logf * LOG2_E), axis=0)

    def hgrn_decay(c):
        hg[c]["decay"] = jnp.exp2(_dot_nn(sums_ref[...], hg[c].pop("logf3")))

    def hgrn_scores(c):
        decay = hg[c]["decay"]
        hg[c]["scores"] = []
        for pair in range(n_pairs):
            ps = slice(pair * pair_width, (pair + 1) * pair_width)
            q2 = hg[c]["q"][:, ps]
            k2 = hg[c]["k"][:, ps]
            scores = masks_ref[n_lvl] * _dot_nt(q2.astype(BF16), _block_diag2(k2.astype(BF16)))
            for li in range(n_lvl):
                d_l = decay[li * CHUNK:(li + 1) * CHUNK, ps]
                scores = scores + masks_ref[li] * _dot_nt(
                    (q2 * d_l).astype(BF16), _block_diag2((k2 * d_l).astype(BF16)))
            hg[c]["scores"].append(scores.astype(BF16))

    def hgrn_out(c):
        rows = slice(c * CHUNK, (c + 1) * CHUNK)
        decay = hg[c].pop("decay")
        qin = hg[c].pop("q")
        kin = hg[c].pop("k")
        gx = proj_s[rows, OFF_HG:OFF_HG + HGRN_WIDTH]
        gate = gx * _sigmoid(gx)
        for pair in range(n_pairs):
            ps = slice(pair * pair_width, (pair + 1) * pair_width)
            v2 = proj_s[rows, OFF_HI + pair * pair_width:OFF_HI + (pair + 1) * pair_width].astype(BF16)
            o2 = _dot_nn(hg[c]["scores"][pair], _block_diag2(v2))
            g_cum = decay[n_lvl * CHUNK:(n_lvl + 1) * CHUNK, ps]
            g_rest = decay[(n_lvl + 1) * CHUNK:(n_lvl + 2) * CHUNK, ps]
            qg = (qin[:, ps] * g_cum).astype(BF16)
            kg = (kin[:, ps] * g_rest).astype(BF16)
            for j in range(2):
                head = 2 * pair + j
                ls = slice(j * HGRN_HEAD_DIM, (j + 1) * HGRN_HEAD_DIM)
                hs = slice(head * HGRN_HEAD_DIM, (head + 1) * HGRN_HEAD_DIM)
                state = st_s[head]
                o = o2[:, ls] + _dot_nt(qg[:, ls], state.astype(BF16))
                st_s[head] = state * g_cum[CHUNK - 1:CHUNK, ls] + _dot_tn(v2[:, ls], kg[:, ls])
                o = o * lax.rsqrt(jnp.mean(o * o, axis=-1, keepdims=True) + NORM_EPS) * hnw
                b_s[rows, hs] = (o * gate[:, hs]).astype(b_s.dtype)
        hg[c].clear()

    hgrn_stages = (hgrn_gates, hgrn_decay, hgrn_scores, hgrn_out)

    def merge(half_idx):
        rows = slice(half_idx * (tb // 2), (half_idx + 1) * (tb // 2))
        up_a = _dot_nn(a_s[rows, :], wua_ref[...])
        up_h = _dot_nn(b_s[rows, :], wuh_ref[...])
        merged = (_sigmoid(proj_s[rows, OFF_GA:OFF_GA + D_MODEL]) * up_a
                  + _sigmoid(proj_s[rows, OFF_GH:OFF_GH + D_MODEL]) * up_h)
        y = xr_ref[0, rows, :] + _dot_nn(merged.astype(BF16), wo_ref[...])
        if final:
            y = _rms_norm(y, fnw_ref[...])
        o_ref[0, rows, :] = y

    n_slots = max(n_att + len(att_stages) - 1, 2 * (n_chunks - 1) + len(hgrn_stages))
    loop_pieces = IN_WIDTH // PROJ_PIECE - TAIL_PIECES
    pieces_per_slot = -(-loop_pieces // n_slots)
    rotate_slot = -(-(OFF_Z // PROJ_PIECE) // pieces_per_slot)
    merge_slot = max(n_att // 2 + len(att_stages) - 1, 2 * (n_chunks // 2 - 1) + len(hgrn_stages))
    for slot in range(n_slots):
        for s, stage in enumerate(att_stages):
            if 0 <= slot - s < n_att:
                stage(slot - s)
        for s, stage in enumerate(hgrn_stages):
            if (slot - s) % 2 == 0 and 0 <= (slot - s) // 2 < n_chunks:
                stage((slot - s) // 2)
        project(max(0, min(pieces_per_slot, loop_pieces - slot * pieces_per_slot)))
        if slot == rotate_slot:
            rotate_new()
        if slot == rotate_slot + 2:
            normalise_next()
        if slot == merge_slot:
            merge(0)
    project(IN_WIDTH // PROJ_PIECE)
    merge(1)


def _layer_call(x, cos_tab, sin_tab, sinks, norm_w, w_in, hgrn_norm_w, w_up_attn, w_up_hgrn, w_out,
                lb_logits, sums, masks, final_norm_w, *, layer, final):
    batch, seq, d = x.shape
    tb = TOKEN_BLOCK
    n_blocks = batch * seq // tb
    once = pl.Buffered(1)
    const2 = lambda g: (0, 0)
    const3 = lambda g: (0, 0, 0)
    blk_next = lambda g: (jnp.minimum(g + 1, n_blocks - 1), 0, 0)
    blk_new = lambda g: (jnp.minimum(g, n_blocks - 1), 0, 0)
    blk_old = lambda g: (jnp.maximum(g - 1, 0), 0, 0)
    x_blocks = x.reshape(n_blocks, tb, d)
    out = pl.pallas_call(
        functools.partial(_layer_kernel, layer=layer, final=final, blocks_per_seq=seq // tb),
        grid=(n_blocks + 1,),
        in_specs=[
            pl.BlockSpec(memory_space=pltpu.SMEM),
            pl.BlockSpec((1, tb, d), blk_next),
            pl.BlockSpec((1, tb, d), blk_old),
            pl.BlockSpec((1, tb, LANES), blk_new),
            pl.BlockSpec((1, tb, LANES), blk_new),
            pl.BlockSpec((1, d), const2, pipeline_mode=once),
            pl.BlockSpec((d, IN_WIDTH), const2, pipeline_mode=once),
            pl.BlockSpec((1, HGRN_HEAD_DIM), const2, pipeline_mode=once),
            pl.BlockSpec((ATTN_WIDTH, d), const2, pipeline_mode=once),
            pl.BlockSpec((HGRN_WIDTH, d), const2, pipeline_mode=once),
            pl.BlockSpec((d, d), const2, pipeline_mode=once),
            pl.BlockSpec((DEPTH, HGRN_WIDTH), const2, pipeline_mode=once),
            pl.BlockSpec((N_SUMS * CHUNK, 3 * CHUNK), const2, pipeline_mode=once),
            pl.BlockSpec((len(LEVELS) + 1, CHUNK, 2 * CHUNK), const3, pipeline_mode=once),
            pl.BlockSpec((1, d), const2, pipeline_mode=once),
        ],
        out_specs=pl.BlockSpec((1, tb, d), blk_old),
        out_shape=jax.ShapeDtypeStruct((n_blocks, tb, d), x.dtype),
        scratch_shapes=[
            pltpu.VMEM((tb, IN_WIDTH), F32),
            pltpu.VMEM((tb, IN_WIDTH), F32),
            pltpu.VMEM((tb, d), BF16),
            pltpu.VMEM((tb, d), BF16),
            pltpu.VMEM((tb, ATTN_WIDTH), BF16),
            pltpu.VMEM((tb, ATTN_WIDTH), BF16),
            pltpu.VMEM((WINDOW + tb, KV_WIDTH), BF16),
            pltpu.VMEM((WINDOW + tb, KV_WIDTH), BF16),
            pltpu.VMEM((WINDOW + tb, KV_WIDTH), BF16),
            pltpu.VMEM((WINDOW + tb, KV_WIDTH), BF16),
            pltpu.VMEM((HGRN_HEADS, HGRN_HEAD_DIM, HGRN_HEAD_DIM), F32),
            pltpu.VMEM((tb, ATTN_WIDTH), BF16),
            pltpu.VMEM((tb, HGRN_WIDTH), BF16),
        ],
        compiler_params=pltpu.CompilerParams(
            dimension_semantics=("arbitrary",),
            vmem_limit_bytes=VMEM_LIMIT_BYTES),
        name=f"hybrid_layer_{layer}",
    )(sinks, x_blocks, x_blocks, cos_tab.reshape(n_blocks, tb, LANES),
      sin_tab.reshape(n_blocks, tb, LANES), norm_w, w_in, hgrn_norm_w, w_up_attn, w_up_hgrn, w_out,
      lb_logits, sums, masks, final_norm_w)
    return out.reshape(batch, seq, d)


def kernel(x, positions, norm_w, w_in, attn_sinks, hgrn_norm_w, w_up_attn, w_up_hgrn, w_out, lb_logits,
           final_norm_w):
    depth = w_in.shape[0]
    assert depth == DEPTH and x.shape[1] % TOKEN_BLOCK == 0 and x.shape[1] % ROPE_BLOCK == 0
    cos_tab, sin_tab = _rope_tables(positions)
    sums_np, masks_np = _hgrn_constants()
    sums = jnp.asarray(sums_np, dtype=BF16)
    masks = jnp.asarray(masks_np, dtype=F32)
    fnw = final_norm_w.reshape(1, D_MODEL)
    for layer in range(depth):
        x = _layer_call(
            x, cos_tab, sin_tab, attn_sinks[layer], norm_w[layer].reshape(1, D_MODEL),
            w_in[layer].astype(BF16), hgrn_norm_w[layer].reshape(1, HGRN_HEAD_DIM),
            w_up_attn[layer].astype(BF16), w_up_hgrn[layer].astype(BF16), w_out[layer].astype(BF16),
            lb_logits, sums, masks, fnw, layer=layer, final=(layer == depth - 1))
    return x
```

```python
import functools

import numpy as np
import jax
import jax.numpy as jnp
from jax import lax
from jax.experimental import pallas as pl
from jax.experimental.pallas import tpu as pltpu

D_MODEL = 1024
DEPTH = 2
N_HEADS = 8
N_KV_HEADS = 2
HEAD_DIM = 64
ATTN_WIDTH = N_HEADS * HEAD_DIM
KV_WIDTH = N_KV_HEADS * HEAD_DIM
WINDOW = 128
ROT_DIM = HEAD_DIM // 4
ROPE_THETA = 500000.0
HGRN_HEADS = 4
HGRN_HEAD_DIM = 128
HGRN_WIDTH = HGRN_HEADS * HGRN_HEAD_DIM
CHUNK = 64
NORM_EPS = 1e-6
MASK_VALUE = -1e30
LOG2_E = 1.4426950408889634
IN_WIDTH = 2 * ATTN_WIDTH + 2 * KV_WIDTH + 4 * HGRN_WIDTH + 2 * D_MODEL

OFF_Q = 0
OFF_K = OFF_Q + ATTN_WIDTH
OFF_V = OFF_K + KV_WIDTH
OFF_Z = OFF_V + KV_WIDTH
OFF_HQ = OFF_Z + ATTN_WIDTH
OFF_HF = OFF_HQ + HGRN_WIDTH
OFF_HI = OFF_HF + HGRN_WIDTH
OFF_HG = OFF_HI + HGRN_WIDTH
OFF_GA = OFF_HG + HGRN_WIDTH
OFF_GH = OFF_GA + D_MODEL

LANES = 128
TOKEN_BLOCK = 128
ROPE_BLOCK = 512
PROJ_PIECE = 256
TAIL_PIECES = 5
VMEM_LIMIT_BYTES = 56 * 1024 * 1024

LEVELS = (32, 16, 8, 4, 2, 1)
N_SUMS = len(LEVELS) + 2

F32 = jnp.float32
BF16 = jnp.bfloat16


def _dot_nn(a, b):
    return lax.dot_general(a, b, (((1,), (0,)), ((), ())), preferred_element_type=F32)


def _dot_nt(a, b):
    return lax.dot_general(a, b, (((1,), (1,)), ((), ())), preferred_element_type=F32)


def _dot_tn(a, b):
    return lax.dot_general(a, b, (((0,), (0,)), ((), ())), preferred_element_type=F32)


def _split3(x):
    hi = x.astype(BF16)
    r1 = x - hi.astype(F32)
    mid = r1.astype(BF16)
    lo = (r1 - mid.astype(F32)).astype(BF16)
    return hi, mid, lo


def _sigmoid(x):
    return 0.5 * jnp.tanh(0.5 * x) + 0.5


def _block_diag2(a):
    zero = jnp.zeros((a.shape[0], LANES), a.dtype)
    top = jnp.concatenate([a[:, :LANES], zero], axis=1)
    bottom = jnp.concatenate([zero, a[:, LANES:]], axis=1)
    return jnp.concatenate([top, bottom], axis=0)


def _hgrn_constants():
    t = np.arange(CHUNK)[:, None]
    u = np.arange(CHUNK)[None, :]
    sums, masks = [], []
    for h in LEVELS:
        right = (t // h) % 2 == 1
        m = (t // (2 * h)) * 2 * h + h - 1
        sums.append(np.where(right, (u > m) & (u <= t), (u > t) & (u <= m)))
        masks.append((t // (2 * h) == u // (2 * h)) & right & ((u // h) % 2 == 0))
    sums.append(u <= t)
    sums.append(u > t)
    masks.append(t == u)
    sums = np.concatenate(sums, axis=0).astype(np.float32)
    masks = np.stack(masks, axis=0).astype(np.float32)
    return np.concatenate([sums] * 3, axis=1), np.concatenate([masks] * 2, axis=2)


def _rope_expand_matrix():
    half = ROT_DIM // 2
    e = np.zeros((2 * half, 2 * LANES), np.float32)
    for lane in range(LANES):
        d = lane % HEAD_DIM
        if d < half:
            e[d, lane] = 1.0
            e[half + d, LANES + lane] = -1.0
        elif d < ROT_DIM:
            e[d - half, lane] = 1.0
            e[half + d - half, LANES + lane] = 1.0
    return e


def _rope_table_kernel(pos_ref, invf_ref, expand_ref, cos_ref, sin_ref):
    batch = pos_ref.shape[0]
    lane = lax.broadcasted_iota(jnp.int32, (1, LANES), 1) % HEAD_DIM
    passthrough = jnp.where(lane >= ROT_DIM, 1.0, 0.0).astype(F32)
    expand = expand_ref[...]
    for b in range(batch):
        pos = pos_ref[b:b + 1, :].astype(F32)
        ang = invf_ref[...] * pos
        cs = jnp.concatenate([jnp.cos(ang), jnp.sin(ang)], axis=0)
        out = None
        for piece in _split3(cs):
            part = _dot_tn(piece, expand)
            out = part if out is None else out + part
        cos_ref[b] = out[:, :LANES] + passthrough
        sin_ref[b] = out[:, LANES:]


def _rope_tables(positions):
    batch, seq = positions.shape
    half = ROT_DIM // 2
    inv_freq = jnp.power(ROPE_THETA, -jnp.arange(half, dtype=F32) * (2.0 / ROT_DIM)).reshape(half, 1)
    expand = jnp.asarray(_rope_expand_matrix(), dtype=BF16)
    out_sds = jax.ShapeDtypeStruct((batch, seq, LANES), F32)
    return pl.pallas_call(
        _rope_table_kernel,
        grid=(seq // ROPE_BLOCK,),
        in_specs=[
            pl.BlockSpec((batch, ROPE_BLOCK), lambda t: (0, t)),
            pl.BlockSpec((half, 1), lambda t: (0, 0)),
            pl.BlockSpec((2 * half, 2 * LANES), lambda t: (0, 0)),
        ],
        out_specs=[
            pl.BlockSpec((batch, ROPE_BLOCK, LANES), lambda t: (0, t, 0)),
            pl.BlockSpec((batch, ROPE_BLOCK, LANES), lambda t: (0, t, 0)),
        ],
        out_shape=[out_sds, out_sds],
        name="rope_tables",
    )(positions, inv_freq, expand)


def _rms_norm(x, w):
    return x * lax.rsqrt(jnp.mean(x * x, axis=-1, keepdims=True) + NORM_EPS) * w


def _layer_kernel(sinks_ref, xn_ref, xr_ref, cos_ref, sin_ref, nw_ref, win_ref, hnw_ref, wua_ref, wuh_ref,
                  wo_ref, lbl_ref, sums_ref, masks_ref, fnw_ref, o_ref,
                  proj0_s, proj1_s, h0_s, h1_s, q0_s, q1_s, kb0_s, kb1_s, vb0_s, vb1_s, st_s, a_s, b_s,
                  *, layer, final, blocks_per_seq):
    g = pl.program_id(0)
    t_idx = (g + blocks_per_seq - 1) % blocks_per_seq

    @pl.when(g == 0)
    def _():
        for ref in (proj1_s, q1_s, kb0_s, kb1_s, vb0_s, vb1_s):
            ref[...] = jnp.zeros(ref.shape, ref.dtype)
        h0_s[...] = _rms_norm(xr_ref[0], nw_ref[...]).astype(BF16)

    @pl.when(jnp.logical_or(t_idx == 0, g == 0))
    def _():
        st_s[...] = jnp.zeros(st_s.shape, st_s.dtype)

    step = functools.partial(
        _layer_step, sinks_ref, xn_ref, xr_ref, cos_ref, sin_ref, nw_ref, win_ref, hnw_ref, wua_ref,
        wuh_ref, wo_ref, lbl_ref, sums_ref, masks_ref, fnw_ref, o_ref, st_s, a_s, b_s,
        t_idx=t_idx, layer=layer, final=final)
    even = dict(proj=proj0_s, h=h0_s, q=q0_s, kb=kb0_s, vb=vb0_s)
    odd = dict(proj=proj1_s, h=h1_s, q=q1_s, kb=kb1_s, vb=vb1_s)

    @pl.when(g % 2 == 0)
    def _():
        step(even, odd)

    @pl.when(g % 2 == 1)
    def _():
        step(odd, even)


def _layer_step(sinks_ref, xn_ref, xr_ref, cos_ref, sin_ref, nw_ref, win_ref, hnw_ref, wua_ref, wuh_ref,
                wo_ref, lbl_ref, sums_ref, masks_ref, fnw_ref, o_ref, st_s, a_s, b_s, new, old,
                *, t_idx, layer, final):
    tb = xr_ref.shape[1]
    proj_s, q_s, kb_s, vb_s = old["proj"], old["q"], old["kb"], old["vb"]

    kb_s[0:WINDOW, :] = new["kb"][tb:tb + WINDOW, :]
    vb_s[0:WINDOW, :] = new["vb"][tb:tb + WINDOW, :]

    pieces = iter(range(0, IN_WIDTH, PROJ_PIECE))

    def project(n_pieces):
        for _ in range(n_pieces):
            c0 = next(pieces, None)
            if c0 is not None:
                new["proj"][:, c0:c0 + PROJ_PIECE] = _dot_nn(new["h"][...], win_ref[:, c0:c0 + PROJ_PIECE])

    def rotate_new():
        half = ROT_DIM // 2
        lane = lax.broadcasted_iota(jnp.int32, (1, LANES), 1) % HEAD_DIM
        first_half = lane < half
        cos_t = cos_ref[0]
        sin_t = sin_ref[0]

        def rope(tile):
            partner = jnp.where(first_half, pltpu.roll(tile, LANES - half, axis=1),
                                pltpu.roll(tile, half, axis=1))
            return tile * cos_t + partner * sin_t

        scale = HEAD_DIM ** -0.5 * LOG2_E
        for j in range(ATTN_WIDTH // LANES):
            new["q"][:, j * LANES:(j + 1) * LANES] = (
                rope(new["proj"][:, OFF_Q + j * LANES:OFF_Q + (j + 1) * LANES]) * scale).astype(BF16)
        for j in range(KV_WIDTH // LANES):
            new["kb"][WINDOW:, j * LANES:(j + 1) * LANES] = rope(
                new["proj"][:, OFF_K + j * LANES:OFF_K + (j + 1) * LANES]).astype(BF16)
        new["vb"][WINDOW:, :] = new["proj"][:, OFF_V:OFF_V + KV_WIDTH].astype(BF16)

    def normalise_next():
        old["h"][...] = _rms_norm(xn_ref[0], nw_ref[...]).astype(BF16)

    row = lax.broadcasted_iota(jnp.int32, (2 * WINDOW, 2 * WINDOW), 0) % WINDOW
    col = lax.broadcasted_iota(jnp.int32, (2 * WINDOW, 2 * WINDOW), 1)
    band = (col > row) & (col <= row + WINDOW)
    upper_rows = lax.broadcasted_iota(jnp.int32, (2 * WINDOW, 1), 0) < WINDOW
    group = N_HEADS // N_KV_HEADS
    pairs_per_block = N_HEADS // 2
    n_att = (tb // WINDOW) * pairs_per_block
    att = [dict() for _ in range(n_att)]

    def att_scores(k):
        i, pr = divmod(k, pairs_per_block)
        heads = (2 * pr, 2 * pr + 1)
        kv = heads[0] // group
        q2 = jnp.concatenate(
            [q_s[i * WINDOW:(i + 1) * WINDOW, h * HEAD_DIM:(h + 1) * HEAD_DIM] for h in heads], axis=0)
        kh = kb_s[i * WINDOW:(i + 2) * WINDOW, kv * HEAD_DIM:(kv + 1) * HEAD_DIM]
        has_prev = jnp.logical_or(t_idx > 0, i > 0)
        mask = band & jnp.logical_or(col >= WINDOW, has_prev)
        att[k]["s"] = jnp.where(mask, _dot_nt(q2, kh), MASK_VALUE)
        att[k]["sink"] = jnp.where(upper_rows, sinks_ref[heads[0]], sinks_ref[heads[1]]) * LOG2_E

    def att_max(k):
        att[k]["m"] = jnp.maximum(jnp.max(att[k]["s"], axis=-1, keepdims=True), att[k]["sink"])

    def att_probs(k):
        m = att[k]["m"]
        p = jnp.exp2(att[k].pop("s") - m)
        att[k]["rdenom"] = 1.0 / (jnp.sum(p, axis=-1, keepdims=True) + jnp.exp2(att[k].pop("sink") - m))
        att[k]["p"] = p.astype(BF16)

    def att_out(k):
        i, pr = divmod(k, pairs_per_block)
        kv = (2 * pr) // group
        vh = vb_s[i * WINDOW:(i + 2) * WINDOW, kv * HEAD_DIM:(kv + 1) * HEAD_DIM]
        o = _dot_nn(att[k].pop("p"), vh) * att[k].pop("rdenom")
        o = jnp.concatenate([o[:WINDOW], o[WINDOW:]], axis=1)
        z = proj_s[i * WINDOW:(i + 1) * WINDOW, OFF_Z + pr * LANES:OFF_Z + (pr + 1) * LANES]
        a_s[i * WINDOW:(i + 1) * WINDOW, pr * LANES:(pr + 1) * LANES] = (o * (z * _sigmoid(z))).astype(a_s.dtype)

    att_stages = (att_scores, att_max, att_probs, att_out)

    lbl = lbl_ref[...]
    lb_e = jnp.exp(lbl - jnp.max(lbl, axis=0, keepdims=True))
    lb_soft = lb_e / jnp.sum(lb_e, axis=0, keepdims=True)
    lb = jnp.sum(lb_soft[0:layer + 1, :], axis=0, keepdims=True) - lb_soft[0:1, :]
    one_minus_lb = 1.0 - lb
    hnw = hnw_ref[...]

    n_lvl = len(LEVELS)
    pair_width = 2 * HGRN_HEAD_DIM
    n_chunks = tb // CHUNK
    n_pairs = HGRN_HEADS // 2
    hg = [dict() for _ in range(n_chunks)]

    def hgrn_gates(c):
        rows = slice(c * CHUNK, (c + 1) * CHUNK)
        fx = proj_s[rows, OFF_HF:OFF_HF + HGRN_WIDTH]
        e = jnp.exp(-jnp.abs(fx))
        r = 1.0 / (1.0 + e)
        er = e * r
        pos_f = fx >= 0
        logf = jnp.log(lb + one_minus_lb * jnp.where(pos_f, r, er))
        hg[c]["k"] = one_minus_lb * jnp.where(pos_f, er, r)
        qx = proj_s[rows, OFF_HQ:OFF_HQ + HGRN_WIDTH]
        hg[c]["q"] = qx * _sigmoid(qx)
        hg[c]["logf3"] = jnp.concatenate(_split3(logf * LOG2_E), axis=0)

    def hgrn_decay(c):
        hg[c]["decay"] = jnp.exp2(_dot_nn(sums_ref[...], hg[c].pop("logf3")))

    def hgrn_scores(c):
        decay = hg[c]["decay"]
        hg[c]["scores"] = []
        for pair in range(n_pairs):
            ps = slice(pair * pair_width, (pair + 1) * pair_width)
            q2 = hg[c]["q"][:, ps]
            k2 = hg[c]["k"][:, ps]
            scores = masks_ref[n_lvl] * _dot_nt(q2.astype(BF16), _block_diag2(k2.astype(BF16)))
            for li in range(n_lvl):
                d_l = decay[li * CHUNK:(li + 1) * CHUNK, ps]
                scores = scores + masks_ref[li] * _dot_nt(
                    (q2 * d_l).astype(BF16), _block_diag2((k2 * d_l).astype(BF16)))
            hg[c]["scores"].append(scores.astype(BF16))

    def hgrn_out(c):
        rows = slice(c * CHUNK, (c + 1) * CHUNK)
        decay = hg[c].pop("decay")
        qin = hg[c].pop("q")
        kin = hg[c].pop("k")
        gx = proj_s[rows, OFF_HG:OFF_HG + HGRN_WIDTH]
        gate = gx * _sigmoid(gx)
        for pair in range(n_pairs):
            ps = slice(pair * pair_width, (pair + 1) * pair_width)
            v2 = proj_s[rows, OFF_HI + pair * pair_width:OFF_HI + (pair + 1) * pair_width].astype(BF16)
            o2 = _dot_nn(hg[c]["scores"][pair], _block_diag2(v2))
            g_cum = decay[n_lvl * CHUNK:(n_lvl + 1) * CHUNK, ps]
            g_rest = decay[(n_lvl + 1) * CHUNK:(n_lvl + 2) * CHUNK, ps]
            qg = (qin[:, ps] * g_cum).astype(BF16)
            kg = (kin[:, ps] * g_rest).astype(BF16)
            for j in range(2):
                head = 2 * pair + j
                ls = slice(j * HGRN_HEAD_DIM, (j + 1) * HGRN_HEAD_DIM)
                hs = slice(head * HGRN_HEAD_DIM, (head + 1) * HGRN_HEAD_DIM)
                state = st_s[head]
                o = o2[:, ls] + _dot_nt(qg[:, ls], state.astype(BF16))
                st_s[head] = state * g_cum[CHUNK - 1:CHUNK, ls] + _dot_tn(v2[:, ls], kg[:, ls])
                o = o * lax.rsqrt(jnp.mean(o * o, axis=-1, keepdims=True) + NORM_EPS) * hnw
                b_s[rows, hs] = (o * gate[:, hs]).astype(b_s.dtype)
        hg[c].clear()

    hgrn_stages = (hgrn_gates, hgrn_decay, hgrn_scores, hgrn_out)

    def merge(half_idx):
        rows = slice(half_idx * (tb // 2), (half_idx + 1) * (tb // 2))
        up_a = _dot_nn(a_s[rows, :], wua_ref[...])
        up_h = _dot_nn(b_s[rows, :], wuh_ref[...])
        merged = (_sigmoid(proj_s[rows, OFF_GA:OFF_GA + D_MODEL]) * up_a
                  + _sigmoid(proj_s[rows, OFF_GH:OFF_GH + D_MODEL]) * up_h)
        y = xr_ref[0, rows, :] + _dot_nn(merged.astype(BF16), wo_ref[...])
        if final:
            y = _rms_norm(y, fnw_ref[...])
        o_ref[0, rows, :] = y

    n_slots = max(n_att + len(att_stages) - 1, 2 * (n_chunks - 1) + len(hgrn_stages))
    loop_pieces = IN_WIDTH // PROJ_PIECE - TAIL_PIECES
    pieces_per_slot = -(-loop_pieces // n_slots)
    rotate_slot = -(-(OFF_Z // PROJ_PIECE) // pieces_per_slot)
    att_half = -(-(tb // 2) // WINDOW) * pairs_per_block
    merge_slot = max(att_half + len(att_stages) - 1, 2 * (n_chunks // 2 - 1) + len(hgrn_stages))
    for slot in range(n_slots):
        for s, stage in enumerate(att_stages):
            if 0 <= slot - s < n_att:
                stage(slot - s)
        for s, stage in enumerate(hgrn_stages):
            if (slot - s) % 2 == 0 and 0 <= (slot - s) // 2 < n_chunks:
                stage((slot - s) // 2)
        project(max(0, min(pieces_per_slot, loop_pieces - slot * pieces_per_slot)))
        if slot == rotate_slot:
            rotate_new()
        if slot == rotate_slot + 2:
            normalise_next()
        if slot == merge_slot:
            merge(0)
    project(IN_WIDTH // PROJ_PIECE)
    if merge_slot >= n_slots:
        merge(0)
    merge(1)


def _layer_call(x, cos_tab, sin_tab, sinks, norm_w, w_in, hgrn_norm_w, w_up_attn, w_up_hgrn, w_out,
                lb_logits, sums, masks, final_norm_w, *, layer, final):
    batch, seq, d = x.shape
    tb = TOKEN_BLOCK
    n_blocks = batch * seq // tb
    once = pl.Buffered(1)
    const2 = lambda g: (0, 0)
    const3 = lambda g: (0, 0, 0)
    blk_next = lambda g: (jnp.minimum(g + 1, n_blocks - 1), 0, 0)
    blk_new = lambda g: (jnp.minimum(g, n_blocks - 1), 0, 0)
    blk_old = lambda g: (jnp.maximum(g - 1, 0), 0, 0)
    x_blocks = x.reshape(n_blocks, tb, d)
    out = pl.pallas_call(
        functools.partial(_layer_kernel, layer=layer, final=final, blocks_per_seq=seq // tb),
        grid=(n_blocks + 1,),
        in_specs=[
            pl.BlockSpec(memory_space=pltpu.SMEM),
            pl.BlockSpec((1, tb, d), blk_next),
            pl.BlockSpec((1, tb, d), blk_old),
            pl.BlockSpec((1, tb, LANES), blk_new),
            pl.BlockSpec((1, tb, LANES), blk_new),
            pl.BlockSpec((1, d), const2, pipeline_mode=once),
            pl.BlockSpec((d, IN_WIDTH), const2, pipeline_mode=once),
            pl.BlockSpec((1, HGRN_HEAD_DIM), const2, pipeline_mode=once),
            pl.BlockSpec((ATTN_WIDTH, d), const2, pipeline_mode=once),
            pl.BlockSpec((HGRN_WIDTH, d), const2, pipeline_mode=once),
            pl.BlockSpec((d, d), const2, pipeline_mode=once),
            pl.BlockSpec((DEPTH, HGRN_WIDTH), const2, pipeline_mode=once),
            pl.BlockSpec((N_SUMS * CHUNK, 3 * CHUNK), const2, pipeline_mode=once),
            pl.BlockSpec((len(LEVELS) + 1, CHUNK, 2 * CHUNK), const3, pipeline_mode=once),
            pl.BlockSpec((1, d), const2, pipeline_mode=once),
        ],
        out_specs=pl.BlockSpec((1, tb, d), blk_old),
        out_shape=jax.ShapeDtypeStruct((n_blocks, tb, d), x.dtype),
        scratch_shapes=[
            pltpu.VMEM((tb, IN_WIDTH), F32),
            pltpu.VMEM((tb, IN_WIDTH), F32),
            pltpu.VMEM((tb, d), BF16),
            pltpu.VMEM((tb, d), BF16),
            pltpu.VMEM((tb, ATTN_WIDTH), BF16),
            pltpu.VMEM((tb, ATTN_WIDTH), BF16),
            pltpu.VMEM((WINDOW + tb, KV_WIDTH), BF16),
            pltpu.VMEM((WINDOW + tb, KV_WIDTH), BF16),
            pltpu.VMEM((WINDOW + tb, KV_WIDTH), BF16),
            pltpu.VMEM((WINDOW + tb, KV_WIDTH), BF16),
            pltpu.VMEM((HGRN_HEADS, HGRN_HEAD_DIM, HGRN_HEAD_DIM), F32),
            pltpu.VMEM((tb, ATTN_WIDTH), BF16),
            pltpu.VMEM((tb, HGRN_WIDTH), BF16),
        ],
        compiler_params=pltpu.CompilerParams(
            dimension_semantics=("arbitrary",),
            vmem_limit_bytes=VMEM_LIMIT_BYTES),
        name=f"hybrid_layer_{layer}",
    )(sinks, x_blocks, x_blocks, cos_tab.reshape(n_blocks, tb, LANES),
      sin_tab.reshape(n_blocks, tb, LANES), norm_w, w_in, hgrn_norm_w, w_up_attn, w_up_hgrn, w_out,
      lb_logits, sums, masks, final_norm_w)
    return out.reshape(batch, seq, d)


def kernel(x, positions, norm_w, w_in, attn_sinks, hgrn_norm_w, w_up_attn, w_up_hgrn, w_out, lb_logits,
           final_norm_w):
    depth = w_in.shape[0]
    assert depth == DEPTH and x.shape[1] % TOKEN_BLOCK == 0 and x.shape[1] % ROPE_BLOCK == 0
    cos_tab, sin_tab = _rope_tables(positions)
    sums_np, masks_np = _hgrn_constants()
    sums = jnp.asarray(sums_np, dtype=BF16)
    masks = jnp.asarray(masks_np, dtype=F32)
    fnw = final_norm_w.reshape(1, D_MODEL)
    for layer in range(depth):
        x = _layer_call(
            x, cos_tab, sin_tab, attn_sinks[layer], norm_w[layer].reshape(1, D_MODEL),
            w_in[layer].astype(BF16), hgrn_norm_w[layer].reshape(1, HGRN_HEAD_DIM),
            w_up_attn[layer].astype(BF16), w_up_hgrn[layer].astype(BF16), w_out[layer].astype(BF16),
            lb_logits, sums, masks, fnw, layer=layer, final=(layer == depth - 1))
    return x
```

```python
import functools

import numpy as np
import jax
import jax.numpy as jnp
from jax import lax
from jax.experimental import pallas as pl
from jax.experimental.pallas import tpu as pltpu

D_MODEL = 1024
DEPTH = 2
N_HEADS = 8
N_KV_HEADS = 2
HEAD_DIM = 64
ATTN_WIDTH = N_HEADS * HEAD_DIM
KV_WIDTH = N_KV_HEADS * HEAD_DIM
WINDOW = 128
ROT_DIM = HEAD_DIM // 4
ROPE_THETA = 500000.0
HGRN_HEADS = 4
HGRN_HEAD_DIM = 128
HGRN_WIDTH = HGRN_HEADS * HGRN_HEAD_DIM
CHUNK = 64
NORM_EPS = 1e-6
MASK_VALUE = -1e30
LOG2_E = 1.4426950408889634
IN_WIDTH = 2 * ATTN_WIDTH + 2 * KV_WIDTH + 4 * HGRN_WIDTH + 2 * D_MODEL

OFF_Q = 0
OFF_K = OFF_Q + ATTN_WIDTH
OFF_V = OFF_K + KV_WIDTH
OFF_Z = OFF_V + KV_WIDTH
OFF_HQ = OFF_Z + ATTN_WIDTH
OFF_HF = OFF_HQ + HGRN_WIDTH
OFF_HI = OFF_HF + HGRN_WIDTH
OFF_HG = OFF_HI + HGRN_WIDTH
OFF_GA = OFF_HG + HGRN_WIDTH
OFF_GH = OFF_GA + D_MODEL

LANES = 128
TOKEN_BLOCK = 256
ROPE_BLOCK = 512
PROJ_PIECE = 256
TAIL_PIECES = 5
VMEM_LIMIT_BYTES = 56 * 1024 * 1024

LEVELS = (32, 16, 8, 4, 2, 1)
N_SUMS = len(LEVELS) + 2

F32 = jnp.float32
BF16 = jnp.bfloat16


def _dot_nn(a, b):
    return lax.dot_general(a, b, (((1,), (0,)), ((), ())), preferred_element_type=F32)


def _dot_nt(a, b):
    return lax.dot_general(a, b, (((1,), (1,)), ((), ())), preferred_element_type=F32)


def _dot_tn(a, b):
    return lax.dot_general(a, b, (((0,), (0,)), ((), ())), preferred_element_type=F32)


def _split3(x):
    hi = x.astype(BF16)
    r1 = x - hi.astype(F32)
    mid = r1.astype(BF16)
    lo = (r1 - mid.astype(F32)).astype(BF16)
    return hi, mid, lo


def _sigmoid(x):
    return 0.5 * jnp.tanh(0.5 * x) + 0.5


def _block_diag2(a):
    zero = jnp.zeros((a.shape[0], LANES), a.dtype)
    top = jnp.concatenate([a[:, :LANES], zero], axis=1)
    bottom = jnp.concatenate([zero, a[:, LANES:]], axis=1)
    return jnp.concatenate([top, bottom], axis=0)


def _hgrn_constants():
    t = np.arange(CHUNK)[:, None]
    u = np.arange(CHUNK)[None, :]
    sums, masks = [], []
    for h in LEVELS:
        right = (t // h) % 2 == 1
        m = (t // (2 * h)) * 2 * h + h - 1
        sums.append(np.where(right, (u > m) & (u <= t), (u > t) & (u <= m)))
        masks.append((t // (2 * h) == u // (2 * h)) & right & ((u // h) % 2 == 0))
    sums.append(u <= t)
    sums.append(u > t)
    masks.append(t == u)
    sums = np.concatenate(sums, axis=0).astype(np.float32)
    masks = np.stack(masks, axis=0).astype(np.float32)
    return np.concatenate([sums] * 3, axis=1), np.concatenate([masks] * 2, axis=2)


def _rope_expand_matrix():
    half = ROT_DIM // 2
    e = np.zeros((2 * half, 2 * LANES), np.float32)
    for lane in range(LANES):
        d = lane % HEAD_DIM
        if d < half:
            e[d, lane] = 1.0
            e[half + d, LANES + lane] = -1.0
        elif d < ROT_DIM:
            e[d - half, lane] = 1.0
            e[half + d - half, LANES + lane] = 1.0
    return e


def _rope_table_kernel(pos_ref, invf_ref, expand_ref, cos_ref, sin_ref):
    batch = pos_ref.shape[0]
    lane = lax.broadcasted_iota(jnp.int32, (1, LANES), 1) % HEAD_DIM
    passthrough = jnp.where(lane >= ROT_DIM, 1.0, 0.0).astype(F32)
    expand = expand_ref[...]
    for b in range(batch):
        pos = pos_ref[b:b + 1, :].astype(F32)
        ang = invf_ref[...] * pos
        cs = jnp.concatenate([jnp.cos(ang), jnp.sin(ang)], axis=0)
        out = None
        for piece in _split3(cs):
            part = _dot_tn(piece, expand)
            out = part if out is None else out + part
        cos_ref[b] = out[:, :LANES] + passthrough
        sin_ref[b] = out[:, LANES:]


def _rope_tables(positions):
    batch, seq = positions.shape
    half = ROT_DIM // 2
    inv_freq = jnp.power(ROPE_THETA, -jnp.arange(half, dtype=F32) * (2.0 / ROT_DIM)).reshape(half, 1)
    expand = jnp.asarray(_rope_expand_matrix(), dtype=BF16)
    out_sds = jax.ShapeDtypeStruct((batch, seq, LANES), F32)
    return pl.pallas_call(
        _rope_table_kernel,
        grid=(seq // ROPE_BLOCK,),
        in_specs=[
            pl.BlockSpec((batch, ROPE_BLOCK), lambda t: (0, t)),
            pl.BlockSpec((half, 1), lambda t: (0, 0)),
            pl.BlockSpec((2 * half, 2 * LANES), lambda t: (0, 0)),
        ],
        out_specs=[
            pl.BlockSpec((batch, ROPE_BLOCK, LANES), lambda t: (0, t, 0)),
            pl.BlockSpec((batch, ROPE_BLOCK, LANES), lambda t: (0, t, 0)),
        ],
        out_shape=[out_sds, out_sds],
        name="rope_tables",
    )(positions, inv_freq, expand)


def _rms_norm(x, w):
    return x * lax.rsqrt(jnp.mean(x * x, axis=-1, keepdims=True) + NORM_EPS) * w


def _layer_kernel(sinks_ref, xn_ref, xr_ref, cos_a_ref, sin_a_ref, cos_b_ref, sin_b_ref, nw_ref, win_ref,
                  hnw_ref, wua_ref, wuh_ref, wo_ref, lbl_ref, sums_ref, masks_ref, fnw_ref, o_ref,
                  proj0_s, proj1_s, h0_s, h1_s, q0_s, q1_s, kb0_s, kb1_s, vb0_s, vb1_s, st_s, a_s, b_s,
                  *, layer, final, blocks_per_seq):
    j = pl.program_id(0)

    @pl.when(j == 0)
    def _():
        for ref in (proj0_s, h1_s, q0_s, kb0_s, kb1_s, vb0_s, vb1_s, st_s):
            ref[...] = jnp.zeros(ref.shape, ref.dtype)

    phase = functools.partial(
        _layer_phase, sinks_ref, xn_ref, xr_ref, nw_ref, win_ref, hnw_ref, wua_ref, wuh_ref, wo_ref,
        lbl_ref, sums_ref, masks_ref, fnw_ref, o_ref, st_s, a_s, b_s, layer=layer, final=final)
    even = dict(proj=proj0_s, h=h0_s, q=q0_s, kb=kb0_s, vb=vb0_s)
    odd = dict(proj=proj1_s, h=h1_s, q=q1_s, kb=kb1_s, vb=vb1_s)
    t_a = (2 * j - 2 + blocks_per_seq) % blocks_per_seq
    t_b = (2 * j - 1 + blocks_per_seq) % blocks_per_seq
    phase(cos_a_ref, sin_a_ref, odd, even, t_idx=t_a, half=0)
    phase(cos_b_ref, sin_b_ref, even, odd, t_idx=t_b, half=1)


def _layer_phase(sinks_ref, xn_ref, xr_ref, nw_ref, win_ref, hnw_ref, wua_ref, wuh_ref, wo_ref,
                 lbl_ref, sums_ref, masks_ref, fnw_ref, o_ref, st_s, a_s, b_s, cos_ref, sin_ref, new, old,
                 *, t_idx, half, layer, final):
    tb = cos_ref.shape[1]
    win0 = half * tb
    proj_s, q_s, kb_s, vb_s = old["proj"], old["q"], old["kb"], old["vb"]

    kb_s[0:WINDOW, :] = new["kb"][tb:tb + WINDOW, :]
    vb_s[0:WINDOW, :] = new["vb"][tb:tb + WINDOW, :]

    pieces = iter(range(0, IN_WIDTH, PROJ_PIECE))

    def project(n_pieces):
        for _ in range(n_pieces):
            c0 = next(pieces, None)
            if c0 is not None:
                new["proj"][:, c0:c0 + PROJ_PIECE] = _dot_nn(new["h"][...], win_ref[:, c0:c0 + PROJ_PIECE])

    def rotate_new():
        half = ROT_DIM // 2
        lane = lax.broadcasted_iota(jnp.int32, (1, LANES), 1) % HEAD_DIM
        first_half = lane < half
        cos_t = cos_ref[0]
        sin_t = sin_ref[0]

        def rope(tile):
            partner = jnp.where(first_half, pltpu.roll(tile, LANES - half, axis=1),
                                pltpu.roll(tile, half, axis=1))
            return tile * cos_t + partner * sin_t

        scale = HEAD_DIM ** -0.5 * LOG2_E
        for j in range(ATTN_WIDTH // LANES):
            new["q"][:, j * LANES:(j + 1) * LANES] = (
                rope(new["proj"][:, OFF_Q + j * LANES:OFF_Q + (j + 1) * LANES]) * scale).astype(BF16)
        for j in range(KV_WIDTH // LANES):
            new["kb"][WINDOW:, j * LANES:(j + 1) * LANES] = rope(
                new["proj"][:, OFF_K + j * LANES:OFF_K + (j + 1) * LANES]).astype(BF16)
        new["vb"][WINDOW:, :] = new["proj"][:, OFF_V:OFF_V + KV_WIDTH].astype(BF16)

    def normalise_next():
        old["h"][...] = _rms_norm(xn_ref[0, win0:win0 + tb, :], nw_ref[...]).astype(BF16)

    row = lax.broadcasted_iota(jnp.int32, (2 * WINDOW, 2 * WINDOW), 0) % WINDOW
    col = lax.broadcasted_iota(jnp.int32, (2 * WINDOW, 2 * WINDOW), 1)
    band = (col > row) & (col <= row + WINDOW)
    upper_rows = lax.broadcasted_iota(jnp.int32, (2 * WINDOW, 1), 0) < WINDOW
    group = N_HEADS // N_KV_HEADS
    pairs_per_block = N_HEADS // 2
    n_att = (tb // WINDOW) * pairs_per_block
    att = [dict() for _ in range(n_att)]

    def att_scores(k):
        i, pr = divmod(k, pairs_per_block)
        heads = (2 * pr, 2 * pr + 1)
        kv = heads[0] // group
        q2 = jnp.concatenate(
            [q_s[i * WINDOW:(i + 1) * WINDOW, h * HEAD_DIM:(h + 1) * HEAD_DIM] for h in heads], axis=0)
        kh = kb_s[i * WINDOW:(i + 2) * WINDOW, kv * HEAD_DIM:(kv + 1) * HEAD_DIM]
        has_prev = jnp.logical_or(t_idx > 0, i > 0)
        mask = band & jnp.logical_or(col >= WINDOW, has_prev)
        att[k]["s"] = jnp.where(mask, _dot_nt(q2, kh), MASK_VALUE)
        att[k]["sink"] = jnp.where(upper_rows, sinks_ref[heads[0]], sinks_ref[heads[1]]) * LOG2_E

    def att_max(k):
        att[k]["m"] = jnp.maximum(jnp.max(att[k]["s"], axis=-1, keepdims=True), att[k]["sink"])

    def att_probs(k):
        m = att[k]["m"]
        p = jnp.exp2(att[k].pop("s") - m)
        att[k]["rdenom"] = 1.0 / (jnp.sum(p, axis=-1, keepdims=True) + jnp.exp2(att[k].pop("sink") - m))
        att[k]["p"] = p.astype(BF16)

    def att_out(k):
        i, pr = divmod(k, pairs_per_block)
        kv = (2 * pr) // group
        vh = vb_s[i * WINDOW:(i + 2) * WINDOW, kv * HEAD_DIM:(kv + 1) * HEAD_DIM]
        o = _dot_nn(att[k].pop("p"), vh) * att[k].pop("rdenom")
        o = jnp.concatenate([o[:WINDOW], o[WINDOW:]], axis=1)
        z = proj_s[i * WINDOW:(i + 1) * WINDOW, OFF_Z + pr * LANES:OFF_Z + (pr + 1) * LANES]
        a_s[i * WINDOW:(i + 1) * WINDOW, pr * LANES:(pr + 1) * LANES] = (o * (z * _sigmoid(z))).astype(a_s.dtype)

    att_stages = (att_scores, att_max, att_probs, att_out)

    lbl = lbl_ref[...]
    lb_e = jnp.exp(lbl - jnp.max(lbl, axis=0, keepdims=True))
    lb_soft = lb_e / jnp.sum(lb_e, axis=0, keepdims=True)
    lb = jnp.sum(lb_soft[0:layer + 1, :], axis=0, keepdims=True) - lb_soft[0:1, :]
    one_minus_lb = 1.0 - lb
    hnw = hnw_ref[...]
    carry_state = jnp.where(t_idx == 0, 0.0, 1.0).astype(F32)

    n_lvl = len(LEVELS)
    pair_width = 2 * HGRN_HEAD_DIM
    n_chunks = tb // CHUNK
    n_pairs = HGRN_HEADS // 2
    hg = [dict() for _ in range(n_chunks)]

    def hgrn_gates(c):
        rows = slice(c * CHUNK, (c + 1) * CHUNK)
        fx = proj_s[rows, OFF_HF:OFF_HF + HGRN_WIDTH]
        e = jnp.exp(-jnp.abs(fx))
        r = 1.0 / (1.0 + e)
        er = e * r
        pos_f = fx >= 0
        logf = jnp.log(lb + one_minus_lb * jnp.where(pos_f, r, er))
        hg[c]["k"] = one_minus_lb * jnp.where(pos_f, er, r)
        qx = proj_s[rows, OFF_HQ:OFF_HQ + HGRN_WIDTH]
        hg[c]["q"] = qx * _sigmoid(qx)
        hg[c]["logf3"] = jnp.concatenate(_split3(logf * LOG2_E), axis=0)

    def hgrn_decay(c):
        hg[c]["decay"] = jnp.exp2(_dot_nn(sums_ref[...], hg[c].pop("logf3")))

    def hgrn_scores(c):
        decay = hg[c]["decay"]
        hg[c]["scores"] = []
        for pair in range(n_pairs):
            ps = slice(pair * pair_width, (pair + 1) * pair_width)
            q2 = hg[c]["q"][:, ps]
            k2 = hg[c]["k"][:, ps]
            scores = masks_ref[n_lvl] * _dot_nt(q2.astype(BF16), _block_diag2(k2.astype(BF16)))
            for li in range(n_lvl):
                d_l = decay[li * CHUNK:(li + 1) * CHUNK, ps]
                scores = scores + masks_ref[li] * _dot_nt(
                    (q2 * d_l).astype(BF16), _block_diag2((k2 * d_l).astype(BF16)))
            hg[c]["scores"].append(scores.astype(BF16))

    def hgrn_out(c):
        rows = slice(c * CHUNK, (c + 1) * CHUNK)
        decay = hg[c].pop("decay")
        qin = hg[c].pop("q")
        kin = hg[c].pop("k")
        gx = proj_s[rows, OFF_HG:OFF_HG + HGRN_WIDTH]
        gate = gx * _sigmoid(gx)
        for pair in range(n_pairs):
            ps = slice(pair * pair_width, (pair + 1) * pair_width)
            v2 = proj_s[rows, OFF_HI + pair * pair_width:OFF_HI + (pair + 1) * pair_width].astype(BF16)
            o2 = _dot_nn(hg[c]["scores"][pair], _block_diag2(v2))
            g_cum = decay[n_lvl * CHUNK:(n_lvl + 1) * CHUNK, ps]
            g_rest = decay[(n_lvl + 1) * CHUNK:(n_lvl + 2) * CHUNK, ps]
            qg = (qin[:, ps] * g_cum).astype(BF16)
            kg = (kin[:, ps] * g_rest).astype(BF16)
            for j in range(2):
                head = 2 * pair + j
                ls = slice(j * HGRN_HEAD_DIM, (j + 1) * HGRN_HEAD_DIM)
                hs = slice(head * HGRN_HEAD_DIM, (head + 1) * HGRN_HEAD_DIM)
                state = st_s[head]
                if c == 0:
                    state = state * carry_state
                o = o2[:, ls] + _dot_nt(qg[:, ls], state.astype(BF16))
                st_s[head] = state * g_cum[CHUNK - 1:CHUNK, ls] + _dot_tn(v2[:, ls], kg[:, ls])
                o = o * lax.rsqrt(jnp.mean(o * o, axis=-1, keepdims=True) + NORM_EPS) * hnw
                b_s[rows, hs] = (o * gate[:, hs]).astype(b_s.dtype)
        hg[c].clear()

    hgrn_stages = (hgrn_gates, hgrn_decay, hgrn_scores, hgrn_out)

    def merge(half_idx):
        rows = slice(half_idx * (tb // 2), (half_idx + 1) * (tb // 2))
        win_rows = slice(win0 + rows.start, win0 + rows.stop)
        up_a = _dot_nn(a_s[rows, :], wua_ref[...])
        up_h = _dot_nn(b_s[rows, :], wuh_ref[...])
        merged = (_sigmoid(proj_s[rows, OFF_GA:OFF_GA + D_MODEL]) * up_a
                  + _sigmoid(proj_s[rows, OFF_GH:OFF_GH + D_MODEL]) * up_h)
        y = xr_ref[0, win_rows, :] + _dot_nn(merged.astype(BF16), wo_ref[...])
        if final:
            y = _rms_norm(y, fnw_ref[...])
        o_ref[0, win_rows, :] = y

    n_slots = max(n_att + len(att_stages) - 1, 2 * (n_chunks - 1) + len(hgrn_stages))
    loop_pieces = IN_WIDTH // PROJ_PIECE - TAIL_PIECES
    pieces_per_slot = -(-loop_pieces // n_slots)
    rotate_slot = -(-(OFF_Z // PROJ_PIECE) // pieces_per_slot)
    att_half = -(-(tb // 2) // WINDOW) * pairs_per_block
    merge_slot = max(att_half + len(att_stages) - 1, 2 * (n_chunks // 2 - 1) + len(hgrn_stages))
    for slot in range(n_slots):
        for s, stage in enumerate(att_stages):
            if 0 <= slot - s < n_att:
                stage(slot - s)
        for s, stage in enumerate(hgrn_stages):
            if (slot - s) % 2 == 0 and 0 <= (slot - s) // 2 < n_chunks:
                stage((slot - s) // 2)
        project(max(0, min(pieces_per_slot, loop_pieces - slot * pieces_per_slot)))
        if slot == rotate_slot:
            rotate_new()
        if slot == rotate_slot + 2:
            normalise_next()
        if slot == merge_slot:
            merge(0)
    project(IN_WIDTH // PROJ_PIECE)
    if merge_slot >= n_slots:
        merge(0)
    merge(1)


def _layer_call(x, cos_tab, sin_tab, sinks, norm_w, w_in, hgrn_norm_w, w_up_attn, w_up_hgrn, w_out,
                lb_logits, sums, masks, final_norm_w, *, layer, final):
    batch, seq, d = x.shape
    tb = TOKEN_BLOCK
    n_blocks = batch * seq // tb
    once = pl.Buffered(1)
    const2 = lambda g: (0, 0)
    const3 = lambda g: (0, 0, 0)
    n_steps = n_blocks // 2 + 1
    pair_next = lambda j: (jnp.minimum(j, n_blocks // 2 - 1), 0, 0)
    pair_old = lambda j: (jnp.maximum(j - 1, 0), 0, 0)
    blk_a = lambda j: (jnp.maximum(2 * j - 1, 0), 0, 0)
    blk_b = lambda j: (jnp.minimum(2 * j, n_blocks - 1), 0, 0)
    x_pairs = x.reshape(n_blocks // 2, 2 * tb, d)
    cos_blocks = cos_tab.reshape(n_blocks, tb, LANES)
    sin_blocks = sin_tab.reshape(n_blocks, tb, LANES)
    out = pl.pallas_call(
        functools.partial(_layer_kernel, layer=layer, final=final, blocks_per_seq=seq // tb),
        grid=(n_steps,),
        in_specs=[
            pl.BlockSpec(memory_space=pltpu.SMEM),
            pl.BlockSpec((1, 2 * tb, d), pair_next),
            pl.BlockSpec((1, 2 * tb, d), pair_old),
            pl.BlockSpec((1, tb, LANES), blk_a),
            pl.BlockSpec((1, tb, LANES), blk_a),
            pl.BlockSpec((1, tb, LANES), blk_b),
            pl.BlockSpec((1, tb, LANES), blk_b),
            pl.BlockSpec((1, d), const2, pipeline_mode=once),
            pl.BlockSpec((d, IN_WIDTH), const2, pipeline_mode=once),
            pl.BlockSpec((1, HGRN_HEAD_DIM), const2, pipeline_mode=once),
            pl.BlockSpec((ATTN_WIDTH, d), const2, pipeline_mode=once),
            pl.BlockSpec((HGRN_WIDTH, d), const2, pipeline_mode=once),
            pl.BlockSpec((d, d), const2, pipeline_mode=once),
            pl.BlockSpec((DEPTH, HGRN_WIDTH), const2, pipeline_mode=once),
            pl.BlockSpec((N_SUMS * CHUNK, 3 * CHUNK), const2, pipeline_mode=once),
            pl.BlockSpec((len(LEVELS) + 1, CHUNK, 2 * CHUNK), const3, pipeline_mode=once),
            pl.BlockSpec((1, d), const2, pipeline_mode=once),
        ],
        out_specs=pl.BlockSpec((1, 2 * tb, d), pair_old),
        out_shape=jax.ShapeDtypeStruct((n_blocks // 2, 2 * tb, d), x.dtype),
        scratch_shapes=[
            pltpu.VMEM((tb, IN_WIDTH), F32),
            pltpu.VMEM((tb, IN_WIDTH), F32),
            pltpu.VMEM((tb, d), BF16),
            pltpu.VMEM((tb, d), BF16),
            pltpu.VMEM((tb, ATTN_WIDTH), BF16),
            pltpu.VMEM((tb, ATTN_WIDTH), BF16),
            pltpu.VMEM((WINDOW + tb, KV_WIDTH), BF16),
            pltpu.VMEM((WINDOW + tb, KV_WIDTH), BF16),
            pltpu.VMEM((WINDOW + tb, KV_WIDTH), BF16),
            pltpu.VMEM((WINDOW + tb, KV_WIDTH), BF16),
            pltpu.VMEM((HGRN_HEADS, HGRN_HEAD_DIM, HGRN_HEAD_DIM), F32),
            pltpu.VMEM((tb, ATTN_WIDTH), BF16),
            pltpu.VMEM((tb, HGRN_WIDTH), BF16),
        ],
        compiler_params=pltpu.CompilerParams(
            dimension_semantics=("arbitrary",),
            vmem_limit_bytes=VMEM_LIMIT_BYTES),
        name=f"hybrid_layer_{layer}",
    )(sinks, x_pairs, x_pairs, cos_blocks, sin_blocks, cos_blocks, sin_blocks, norm_w, w_in, hgrn_norm_w,
      w_up_attn, w_up_hgrn, w_out, lb_logits, sums, masks, final_norm_w)
    return out.reshape(batch, seq, d)


def kernel(x, positions, norm_w, w_in, attn_sinks, hgrn_norm_w, w_up_attn, w_up_hgrn, w_out, lb_logits,
           final_norm_w):
    depth = w_in.shape[0]
    assert depth == DEPTH and x.shape[1] % (2 * TOKEN_BLOCK) == 0 and x.shape[1] % ROPE_BLOCK == 0
    cos_tab, sin_tab = _rope_tables(positions)
    sums_np, masks_np = _hgrn_constants()
    sums = jnp.asarray(sums_np, dtype=BF16)
    masks = jnp.asarray(masks_np, dtype=F32)
    fnw = final_norm_w.reshape(1, D_MODEL)
    for layer in range(depth):
        x = _layer_call(
            x, cos_tab, sin_tab, attn_sinks[layer], norm_w[layer].reshape(1, D_MODEL),
            w_in[layer].astype(BF16), hgrn_norm_w[layer].reshape(1, HGRN_HEAD_DIM),
            w_up_attn[layer].astype(BF16), w_up_hgrn[layer].astype(BF16), w_out[layer].astype(BF16),
            lb_logits, sums, masks, fnw, layer=layer, final=(layer == depth - 1))
    return x
```

```python
import functools

import numpy as np
import jax
import jax.numpy as jnp
from jax import lax
from jax.experimental import pallas as pl
from jax.experimental.pallas import tpu as pltpu

D_MODEL = 1024
DEPTH = 2
N_HEADS = 8
N_KV_HEADS = 2
HEAD_DIM = 64
ATTN_WIDTH = N_HEADS * HEAD_DIM
KV_WIDTH = N_KV_HEADS * HEAD_DIM
WINDOW = 128
ROT_DIM = HEAD_DIM // 4
ROPE_THETA = 500000.0
HGRN_HEADS = 4
HGRN_HEAD_DIM = 128
HGRN_WIDTH = HGRN_HEADS * HGRN_HEAD_DIM
CHUNK = 64
NORM_EPS = 1e-6
MASK_VALUE = -1e30
LOG2_E = 1.4426950408889634
IN_WIDTH = 2 * ATTN_WIDTH + 2 * KV_WIDTH + 4 * HGRN_WIDTH + 2 * D_MODEL

OFF_Q = 0
OFF_K = OFF_Q + ATTN_WIDTH
OFF_V = OFF_K + KV_WIDTH
OFF_Z = OFF_V + KV_WIDTH
OFF_HQ = OFF_Z + ATTN_WIDTH
OFF_HF = OFF_HQ + HGRN_WIDTH
OFF_HI = OFF_HF + HGRN_WIDTH
OFF_HG = OFF_HI + HGRN_WIDTH
OFF_GA = OFF_HG + HGRN_WIDTH
OFF_GH = OFF_GA + D_MODEL

LANES = 128
TOKEN_BLOCK = 256
ROPE_BLOCK = 512
PROJ_PIECE = 256
TAIL_PIECES = 5
VMEM_LIMIT_BYTES = 56 * 1024 * 1024

LEVELS = (32, 16, 8, 4, 2, 1)
N_SUMS = len(LEVELS) + 2

F32 = jnp.float32
BF16 = jnp.bfloat16


def _dot_nn(a, b):
    return lax.dot_general(a, b, (((1,), (0,)), ((), ())), preferred_element_type=F32)


def _dot_nt(a, b):
    return lax.dot_general(a, b, (((1,), (1,)), ((), ())), preferred_element_type=F32)


def _dot_tn(a, b):
    return lax.dot_general(a, b, (((0,), (0,)), ((), ())), preferred_element_type=F32)


def _split3(x):
    hi = x.astype(BF16)
    r1 = x - hi.astype(F32)
    mid = r1.astype(BF16)
    lo = (r1 - mid.astype(F32)).astype(BF16)
    return hi, mid, lo


def _sigmoid(x):
    return 0.5 * jnp.tanh(0.5 * x) + 0.5


def _block_diag2(a):
    zero = jnp.zeros((a.shape[0], LANES), a.dtype)
    top = jnp.concatenate([a[:, :LANES], zero], axis=1)
    bottom = jnp.concatenate([zero, a[:, LANES:]], axis=1)
    return jnp.concatenate([top, bottom], axis=0)


def _hgrn_constants():
    t = np.arange(CHUNK)[:, None]
    u = np.arange(CHUNK)[None, :]
    sums, masks = [], []
    for h in LEVELS:
        right = (t // h) % 2 == 1
        m = (t // (2 * h)) * 2 * h + h - 1
        sums.append(np.where(right, (u > m) & (u <= t), (u > t) & (u <= m)))
        masks.append((t // (2 * h) == u // (2 * h)) & right & ((u // h) % 2 == 0))
    sums.append(u <= t)
    sums.append(u > t)
    masks.append(t == u)
    sums = np.concatenate(sums, axis=0).astype(np.float32)
    masks = np.stack(masks, axis=0).astype(np.float32)
    return np.concatenate([sums] * 3, axis=1), np.concatenate([masks.transpose(0, 2, 1)] * 2, axis=1)


def _rope_expand_matrix():
    half = ROT_DIM // 2
    e = np.zeros((2 * half, 2 * LANES), np.float32)
    for lane in range(LANES):
        d = lane % HEAD_DIM
        if d < half:
            e[d, lane] = 1.0
            e[half + d, LANES + lane] = -1.0
        elif d < ROT_DIM:
            e[d - half, lane] = 1.0
            e[half + d - half, LANES + lane] = 1.0
    return e


def _rope_table_kernel(pos_ref, invf_ref, expand_ref, cos_ref, sin_ref):
    batch = pos_ref.shape[0]
    lane = lax.broadcasted_iota(jnp.int32, (1, LANES), 1) % HEAD_DIM
    passthrough = jnp.where(lane >= ROT_DIM, 1.0, 0.0).astype(F32)
    expand = expand_ref[...]
    for b in range(batch):
        pos = pos_ref[b:b + 1, :].astype(F32)
        ang = invf_ref[...] * pos
        cs = jnp.concatenate([jnp.cos(ang), jnp.sin(ang)], axis=0)
        out = None
        for piece in _split3(cs):
            part = _dot_tn(piece, expand)
            out = part if out is None else out + part
        cos_ref[b] = out[:, :LANES] + passthrough
        sin_ref[b] = out[:, LANES:]


def _rope_tables(positions):
    batch, seq = positions.shape
    half = ROT_DIM // 2
    inv_freq = jnp.power(ROPE_THETA, -jnp.arange(half, dtype=F32) * (2.0 / ROT_DIM)).reshape(half, 1)
    expand = jnp.asarray(_rope_expand_matrix(), dtype=BF16)
    out_sds = jax.ShapeDtypeStruct((batch, seq, LANES), F32)
    return pl.pallas_call(
        _rope_table_kernel,
        grid=(seq // ROPE_BLOCK,),
        in_specs=[
            pl.BlockSpec((batch, ROPE_BLOCK), lambda t: (0, t)),
            pl.BlockSpec((half, 1), lambda t: (0, 0)),
            pl.BlockSpec((2 * half, 2 * LANES), lambda t: (0, 0)),
        ],
        out_specs=[
            pl.BlockSpec((batch, ROPE_BLOCK, LANES), lambda t: (0, t, 0)),
            pl.BlockSpec((batch, ROPE_BLOCK, LANES), lambda t: (0, t, 0)),
        ],
        out_shape=[out_sds, out_sds],
        name="rope_tables",
    )(positions, inv_freq, expand)


def _rms_norm(x, w):
    return x * lax.rsqrt(jnp.mean(x * x, axis=-1, keepdims=True) + NORM_EPS) * w


def _layer_kernel(sinks_ref, xn_ref, xr_ref, cos_a_ref, sin_a_ref, cos_b_ref, sin_b_ref, nw_ref, win_ref,
                  hnw_ref, wua_ref, wuh_ref, wo_ref, lbl_ref, sums_ref, masks_ref, fnw_ref, o_ref,
                  proj0_s, proj1_s, h0_s, h1_s, q0_s, q1_s, kb0_s, kb1_s, vb0_s, vb1_s, st_s, a_s, b_s,
                  *, layer, final, blocks_per_seq):
    j = pl.program_id(0)

    @pl.when(j == 0)
    def _():
        for ref in (proj0_s, h1_s, q0_s, kb0_s, kb1_s, vb0_s, vb1_s, st_s):
            ref[...] = jnp.zeros(ref.shape, ref.dtype)

    phase = functools.partial(
        _layer_phase, sinks_ref, xn_ref, xr_ref, nw_ref, win_ref, hnw_ref, wua_ref, wuh_ref, wo_ref,
        lbl_ref, sums_ref, masks_ref, fnw_ref, o_ref, st_s, a_s, b_s, layer=layer, final=final)
    even = dict(proj=proj0_s, h=h0_s, q=q0_s, kb=kb0_s, vb=vb0_s)
    odd = dict(proj=proj1_s, h=h1_s, q=q1_s, kb=kb1_s, vb=vb1_s)
    t_a = (2 * j - 2 + blocks_per_seq) % blocks_per_seq
    t_b = (2 * j - 1 + blocks_per_seq) % blocks_per_seq
    phase(cos_a_ref, sin_a_ref, odd, even, t_idx=t_a, half=0)
    phase(cos_b_ref, sin_b_ref, even, odd, t_idx=t_b, half=1)


def _layer_phase(sinks_ref, xn_ref, xr_ref, nw_ref, win_ref, hnw_ref, wua_ref, wuh_ref, wo_ref,
                 lbl_ref, sums_ref, masks_ref, fnw_ref, o_ref, st_s, a_s, b_s, cos_ref, sin_ref, new, old,
                 *, t_idx, half, layer, final):
    tb = cos_ref.shape[1]
    win0 = half * tb
    proj_s, q_s, kb_s, vb_s = old["proj"], old["q"], old["kb"], old["vb"]

    kb_s[0:WINDOW, :] = new["kb"][tb:tb + WINDOW, :]
    vb_s[0:WINDOW, :] = new["vb"][tb:tb + WINDOW, :]

    pieces = iter(range(0, IN_WIDTH, PROJ_PIECE))

    def project(n_pieces):
        for _ in range(n_pieces):
            c0 = next(pieces, None)
            if c0 is not None:
                new["proj"][:, c0:c0 + PROJ_PIECE] = _dot_nn(new["h"][...], win_ref[:, c0:c0 + PROJ_PIECE])

    def rotate_new():
        half = ROT_DIM // 2
        lane = lax.broadcasted_iota(jnp.int32, (1, LANES), 1) % HEAD_DIM
        first_half = lane < half
        cos_t = cos_ref[0]
        sin_t = sin_ref[0]

        def rope(tile):
            partner = jnp.where(first_half, pltpu.roll(tile, LANES - half, axis=1),
                                pltpu.roll(tile, half, axis=1))
            return tile * cos_t + partner * sin_t

        scale = HEAD_DIM ** -0.5 * LOG2_E
        for j in range(ATTN_WIDTH // LANES):
            new["q"][:, j * LANES:(j + 1) * LANES] = (
                rope(new["proj"][:, OFF_Q + j * LANES:OFF_Q + (j + 1) * LANES]) * scale).astype(BF16)
        for j in range(KV_WIDTH // LANES):
            new["kb"][WINDOW:, j * LANES:(j + 1) * LANES] = rope(
                new["proj"][:, OFF_K + j * LANES:OFF_K + (j + 1) * LANES]).astype(BF16)
        new["vb"][WINDOW:, :] = new["proj"][:, OFF_V:OFF_V + KV_WIDTH].astype(BF16)

    def normalise_next():
        old["h"][...] = _rms_norm(xn_ref[0, win0:win0 + tb, :], nw_ref[...]).astype(BF16)

    row = lax.broadcasted_iota(jnp.int32, (2 * WINDOW, 2 * WINDOW), 0) % WINDOW
    col = lax.broadcasted_iota(jnp.int32, (2 * WINDOW, 2 * WINDOW), 1)
    band = (col > row) & (col <= row + WINDOW)
    upper_rows = lax.broadcasted_iota(jnp.int32, (2 * WINDOW, 1), 0) < WINDOW
    group = N_HEADS // N_KV_HEADS
    pairs_per_block = N_HEADS // 2
    n_att = (tb // WINDOW) * pairs_per_block
    att = [dict() for _ in range(n_att)]

    def att_scores(k):
        i, pr = divmod(k, pairs_per_block)
        heads = (2 * pr, 2 * pr + 1)
        kv = heads[0] // group
        q2 = jnp.concatenate(
            [q_s[i * WINDOW:(i + 1) * WINDOW, h * HEAD_DIM:(h + 1) * HEAD_DIM] for h in heads], axis=0)
        kh = kb_s[i * WINDOW:(i + 2) * WINDOW, kv * HEAD_DIM:(kv + 1) * HEAD_DIM]
        has_prev = jnp.logical_or(t_idx > 0, i > 0)
        mask = band & jnp.logical_or(col >= WINDOW, has_prev)
        att[k]["s"] = jnp.where(mask, _dot_nt(q2, kh), MASK_VALUE)
        att[k]["sink"] = jnp.where(upper_rows, sinks_ref[heads[0]], sinks_ref[heads[1]]) * LOG2_E

    def att_max(k):
        att[k]["m"] = jnp.maximum(jnp.max(att[k]["s"], axis=-1, keepdims=True), att[k]["sink"])

    def att_probs(k):
        m = att[k]["m"]
        p = jnp.exp2(att[k].pop("s") - m)
        att[k]["rdenom"] = 1.0 / (jnp.sum(p, axis=-1, keepdims=True) + jnp.exp2(att[k].pop("sink") - m))
        att[k]["p"] = p.astype(BF16)

    def att_out(k):
        i, pr = divmod(k, pairs_per_block)
        kv = (2 * pr) // group
        vh = vb_s[i * WINDOW:(i + 2) * WINDOW, kv * HEAD_DIM:(kv + 1) * HEAD_DIM]
        o = _dot_nn(att[k].pop("p"), vh) * att[k].pop("rdenom")
        o = jnp.concatenate([o[:WINDOW], o[WINDOW:]], axis=1)
        z = proj_s[i * WINDOW:(i + 1) * WINDOW, OFF_Z + pr * LANES:OFF_Z + (pr + 1) * LANES]
        a_s[i * WINDOW:(i + 1) * WINDOW, pr * LANES:(pr + 1) * LANES] = (o * (z * _sigmoid(z))).astype(a_s.dtype)

    att_stages = (att_scores, att_max, att_probs, att_out)

    lbl = lbl_ref[...]
    lb_e = jnp.exp(lbl - jnp.max(lbl, axis=0, keepdims=True))
    lb_soft = lb_e / jnp.sum(lb_e, axis=0, keepdims=True)
    lb = jnp.sum(lb_soft[0:layer + 1, :], axis=0, keepdims=True) - lb_soft[0:1, :]
    one_minus_lb = 1.0 - lb
    hnw = hnw_ref[...]
    carry_state = jnp.where(t_idx == 0, 0.0, 1.0).astype(F32)

    n_lvl = len(LEVELS)
    pair_width = 2 * HGRN_HEAD_DIM
    n_chunks = tb // CHUNK
    n_pairs = HGRN_HEADS // 2
    hg = [dict() for _ in range(n_chunks)]

    def hgrn_gates(c):
        rows = slice(c * CHUNK, (c + 1) * CHUNK)
        fx = proj_s[rows, OFF_HF:OFF_HF + HGRN_WIDTH]
        e = jnp.exp(-jnp.abs(fx))
        r = 1.0 / (1.0 + e)
        er = e * r
        pos_f = fx >= 0
        logf = jnp.log(lb + one_minus_lb * jnp.where(pos_f, r, er))
        hg[c]["k"] = one_minus_lb * jnp.where(pos_f, er, r)
        qx = proj_s[rows, OFF_HQ:OFF_HQ + HGRN_WIDTH]
        hg[c]["q"] = qx * _sigmoid(qx)
        hg[c]["logf3"] = jnp.concatenate(_split3(logf * LOG2_E), axis=0)

    def hgrn_decay(c):
        hg[c]["decay"] = jnp.exp2(_dot_nn(sums_ref[...], hg[c].pop("logf3")))

    def hgrn_scores(c):
        decay = hg[c]["decay"]
        hg[c]["scores"] = []
        for pair in range(n_pairs):
            ps = slice(pair * pair_width, (pair + 1) * pair_width)
            q2 = hg[c]["q"][:, ps]
            k2 = hg[c]["k"][:, ps]
            scores = masks_ref[n_lvl] * _dot_nt(_block_diag2(k2.astype(BF16)), q2.astype(BF16))
            for li in range(n_lvl):
                d_l = decay[li * CHUNK:(li + 1) * CHUNK, ps]
                scores = scores + masks_ref[li] * _dot_nt(
                    _block_diag2((k2 * d_l).astype(BF16)), (q2 * d_l).astype(BF16))
            hg[c]["scores"].append(scores.astype(BF16))

    def hgrn_out(c):
        rows = slice(c * CHUNK, (c + 1) * CHUNK)
        decay = hg[c].pop("decay")
        qin = hg[c].pop("q")
        kin = hg[c].pop("k")
        gx = proj_s[rows, OFF_HG:OFF_HG + HGRN_WIDTH]
        gate = gx * _sigmoid(gx)
        for pair in range(n_pairs):
            ps = slice(pair * pair_width, (pair + 1) * pair_width)
            v2 = proj_s[rows, OFF_HI + pair * pair_width:OFF_HI + (pair + 1) * pair_width].astype(BF16)
            o2 = _dot_tn(hg[c]["scores"][pair], _block_diag2(v2))
            g_cum = decay[n_lvl * CHUNK:(n_lvl + 1) * CHUNK, ps]
            g_rest = decay[(n_lvl + 1) * CHUNK:(n_lvl + 2) * CHUNK, ps]
            qg = (qin[:, ps] * g_cum).astype(BF16)
            kg = (kin[:, ps] * g_rest).astype(BF16)
            for j in range(2):
                head = 2 * pair + j
                ls = slice(j * HGRN_HEAD_DIM, (j + 1) * HGRN_HEAD_DIM)
                hs = slice(head * HGRN_HEAD_DIM, (head + 1) * HGRN_HEAD_DIM)
                state = st_s[head]
                if c == 0:
                    state = state * carry_state
                o = o2[:, ls] + _dot_nt(qg[:, ls], state.astype(BF16))
                st_s[head] = state * g_cum[CHUNK - 1:CHUNK, ls] + _dot_tn(v2[:, ls], kg[:, ls])
                o = o * lax.rsqrt(jnp.mean(o * o, axis=-1, keepdims=True) + NORM_EPS) * hnw
                b_s[rows, hs] = (o * gate[:, hs]).astype(b_s.dtype)
        hg[c].clear()

    hgrn_stages = (hgrn_gates, hgrn_decay, hgrn_scores, hgrn_out)

    def merge(half_idx):
        rows = slice(half_idx * (tb // 2), (half_idx + 1) * (tb // 2))
        win_rows = slice(win0 + rows.start, win0 + rows.stop)
        up_a = _dot_nn(a_s[rows, :], wua_ref[...])
        up_h = _dot_nn(b_s[rows, :], wuh_ref[...])
        merged = (_sigmoid(proj_s[rows, OFF_GA:OFF_GA + D_MODEL]) * up_a
                  + _sigmoid(proj_s[rows, OFF_GH:OFF_GH + D_MODEL]) * up_h)
        y = xr_ref[0, win_rows, :] + _dot_nn(merged.astype(BF16), wo_ref[...])
        if final:
            y = _rms_norm(y, fnw_ref[...])
        o_ref[0, win_rows, :] = y

    n_slots = max(n_att + len(att_stages) - 1, 2 * (n_chunks - 1) + len(hgrn_stages))
    loop_pieces = IN_WIDTH // PROJ_PIECE - TAIL_PIECES
    pieces_per_slot = -(-loop_pieces // n_slots)
    rotate_slot = -(-(OFF_Z // PROJ_PIECE) // pieces_per_slot)
    att_half = -(-(tb // 2) // WINDOW) * pairs_per_block
    merge_slot = max(att_half + len(att_stages) - 1, 2 * (n_chunks // 2 - 1) + len(hgrn_stages))
    for slot in range(n_slots):
        slot_pieces = max(0, min(pieces_per_slot, loop_pieces - slot * pieces_per_slot))
        project(slot_pieces // 2)
        for s, stage in enumerate(att_stages):
            if 0 <= slot - s < n_att:
                stage(slot - s)
        project(slot_pieces - slot_pieces // 2)
        for s, stage in enumerate(hgrn_stages):
            if (slot - s) % 2 == 0 and 0 <= (slot - s) // 2 < n_chunks:
                stage((slot - s) // 2)
        if slot == rotate_slot:
            rotate_new()
        if slot == rotate_slot + 2:
            normalise_next()
        if slot == merge_slot:
            merge(0)
    project(IN_WIDTH // PROJ_PIECE)
    if merge_slot >= n_slots:
        merge(0)
    merge(1)


def _layer_call(x, cos_tab, sin_tab, sinks, norm_w, w_in, hgrn_norm_w, w_up_attn, w_up_hgrn, w_out,
                lb_logits, sums, masks, final_norm_w, *, layer, final):
    batch, seq, d = x.shape
    tb = TOKEN_BLOCK
    n_blocks = batch * seq // tb
    once = pl.Buffered(1)
    const2 = lambda g: (0, 0)
    const3 = lambda g: (0, 0, 0)
    n_steps = n_blocks // 2 + 1
    pair_next = lambda j: (jnp.minimum(j, n_blocks // 2 - 1), 0, 0)
    pair_old = lambda j: (jnp.maximum(j - 1, 0), 0, 0)
    blk_a = lambda j: (jnp.maximum(2 * j - 1, 0), 0, 0)
    blk_b = lambda j: (jnp.minimum(2 * j, n_blocks - 1), 0, 0)
    x_pairs = x.reshape(n_blocks // 2, 2 * tb, d)
    cos_blocks = cos_tab.reshape(n_blocks, tb, LANES)
    sin_blocks = sin_tab.reshape(n_blocks, tb, LANES)
    out = pl.pallas_call(
        functools.partial(_layer_kernel, layer=layer, final=final, blocks_per_seq=seq // tb),
        grid=(n_steps,),
        in_specs=[
            pl.BlockSpec(memory_space=pltpu.SMEM),
            pl.BlockSpec((1, 2 * tb, d), pair_next),
            pl.BlockSpec((1, 2 * tb, d), pair_old),
            pl.BlockSpec((1, tb, LANES), blk_a),
            pl.BlockSpec((1, tb, LANES), blk_a),
            pl.BlockSpec((1, tb, LANES), blk_b),
            pl.BlockSpec((1, tb, LANES), blk_b),
            pl.BlockSpec((1, d), const2, pipeline_mode=once),
            pl.BlockSpec((d, IN_WIDTH), const2, pipeline_mode=once),
            pl.BlockSpec((1, HGRN_HEAD_DIM), const2, pipeline_mode=once),
            pl.BlockSpec((ATTN_WIDTH, d), const2, pipeline_mode=once),
            pl.BlockSpec((HGRN_WIDTH, d), const2, pipeline_mode=once),
            pl.BlockSpec((d, d), const2, pipeline_mode=once),
            pl.BlockSpec((DEPTH, HGRN_WIDTH), const2, pipeline_mode=once),
            pl.BlockSpec((N_SUMS * CHUNK, 3 * CHUNK), const2, pipeline_mode=once),
            pl.BlockSpec((len(LEVELS) + 1, 2 * CHUNK, CHUNK), const3, pipeline_mode=once),
            pl.BlockSpec((1, d), const2, pipeline_mode=once),
        ],
        out_specs=pl.BlockSpec((1, 2 * tb, d), pair_old),
        out_shape=jax.ShapeDtypeStruct((n_blocks // 2, 2 * tb, d), x.dtype),
        scratch_shapes=[
            pltpu.VMEM((tb, IN_WIDTH), F32),
            pltpu.VMEM((tb, IN_WIDTH), F32),
            pltpu.VMEM((tb, d), BF16),
            pltpu.VMEM((tb, d), BF16),
            pltpu.VMEM((tb, ATTN_WIDTH), BF16),
            pltpu.VMEM((tb, ATTN_WIDTH), BF16),
            pltpu.VMEM((WINDOW + tb, KV_WIDTH), BF16),
            pltpu.VMEM((WINDOW + tb, KV_WIDTH), BF16),
            pltpu.VMEM((WINDOW + tb, KV_WIDTH), BF16),
            pltpu.VMEM((WINDOW + tb, KV_WIDTH), BF16),
            pltpu.VMEM((HGRN_HEADS, HGRN_HEAD_DIM, HGRN_HEAD_DIM), F32),
            pltpu.VMEM((tb, ATTN_WIDTH), BF16),
            pltpu.VMEM((tb, HGRN_WIDTH), BF16),
        ],
        compiler_params=pltpu.CompilerParams(
            dimension_semantics=("arbitrary",),
            vmem_limit_bytes=VMEM_LIMIT_BYTES),
        name=f"hybrid_layer_{layer}",
    )(sinks, x_pairs, x_pairs, cos_blocks, sin_blocks, cos_blocks, sin_blocks, norm_w, w_in, hgrn_norm_w,
      w_up_attn, w_up_hgrn, w_out, lb_logits, sums, masks, final_norm_w)
    return out.reshape(batch, seq, d)


def kernel(x, positions, norm_w, w_in, attn_sinks, hgrn_norm_w, w_up_attn, w_up_hgrn, w_out, lb_logits,
           final_norm_w):
    depth = w_in.shape[0]
    assert depth == DEPTH and x.shape[1] % (2 * TOKEN_BLOCK) == 0 and x.shape[1] % ROPE_BLOCK == 0
    cos_tab, sin_tab = _rope_tables(positions)
    sums_np, masks_np = _hgrn_constants()
    sums = jnp.asarray(sums_np, dtype=BF16)
    masks = jnp.asarray(masks_np, dtype=F32)
    fnw = final_norm_w.reshape(1, D_MODEL)
    for layer in range(depth):
        x = _layer_call(
            x, cos_tab, sin_tab, attn_sinks[layer], norm_w[layer].reshape(1, D_MODEL),
            w_in[layer].astype(BF16), hgrn_norm_w[layer].reshape(1, HGRN_HEAD_DIM),
            w_up_attn[layer].astype(BF16), w_up_hgrn[layer].astype(BF16), w_out[layer].astype(BF16),
            lb_logits, sums, masks, fnw, layer=layer, final=(layer == depth - 1))
    return x
```

```python
import functools

import numpy as np
import jax
import jax.numpy as jnp
from jax import lax
from jax.experimental import pallas as pl
from jax.experimental.pallas import tpu as pltpu

D_MODEL = 1024
DEPTH = 2
N_HEADS = 8
N_KV_HEADS = 2
HEAD_DIM = 64
ATTN_WIDTH = N_HEADS * HEAD_DIM
KV_WIDTH = N_KV_HEADS * HEAD_DIM
WINDOW = 128
ROT_DIM = HEAD_DIM // 4
ROPE_THETA = 500000.0
HGRN_HEADS = 4
HGRN_HEAD_DIM = 128
HGRN_WIDTH = HGRN_HEADS * HGRN_HEAD_DIM
CHUNK = 64
NORM_EPS = 1e-6
MASK_VALUE = -1e30
LOG2_E = 1.4426950408889634
IN_WIDTH = 2 * ATTN_WIDTH + 2 * KV_WIDTH + 4 * HGRN_WIDTH + 2 * D_MODEL

OFF_Q = 0
OFF_K = OFF_Q + ATTN_WIDTH
OFF_V = OFF_K + KV_WIDTH
OFF_Z = OFF_V + KV_WIDTH
OFF_HQ = OFF_Z + ATTN_WIDTH
OFF_HF = OFF_HQ + HGRN_WIDTH
OFF_HI = OFF_HF + HGRN_WIDTH
OFF_HG = OFF_HI + HGRN_WIDTH
OFF_GA = OFF_HG + HGRN_WIDTH
OFF_GH = OFF_GA + D_MODEL

LANES = 128
TOKEN_BLOCK = 256
ROPE_BLOCK = 512
PROJ_PIECE = 256
TAIL_PIECES = 5
VMEM_LIMIT_BYTES = 56 * 1024 * 1024

LEVELS = (32, 16, 8, 4, 2, 1)

F32 = jnp.float32
BF16 = jnp.bfloat16


def _dot_nn(a, b):
    return lax.dot_general(a, b, (((1,), (0,)), ((), ())), preferred_element_type=F32)


def _dot_nt(a, b):
    return lax.dot_general(a, b, (((1,), (1,)), ((), ())), preferred_element_type=F32)


def _dot_tn(a, b):
    return lax.dot_general(a, b, (((0,), (0,)), ((), ())), preferred_element_type=F32)


def _split3(x):
    hi = x.astype(BF16)
    r1 = x - hi.astype(F32)
    mid = r1.astype(BF16)
    lo = (r1 - mid.astype(F32)).astype(BF16)
    return hi, mid, lo


def _sigmoid(x):
    return 0.5 * jnp.tanh(0.5 * x) + 0.5


def _block_diag2(a):
    zero = jnp.zeros((a.shape[0], LANES), a.dtype)
    top = jnp.concatenate([a[:, :LANES], zero], axis=1)
    bottom = jnp.concatenate([zero, a[:, LANES:]], axis=1)
    return jnp.concatenate([top, bottom], axis=0)


def _hgrn_constants():
    t = np.arange(CHUNK)[:, None]
    u = np.arange(CHUNK)[None, :]
    masks = []
    for h in LEVELS:
        right = (t // h) % 2 == 1
        masks.append((t // (2 * h) == u // (2 * h)) & right & ((u // h) % 2 == 0))
    masks.append(t == u)
    masks = np.stack(masks, axis=0).astype(np.float32)
    tril = (u <= t).astype(np.float32)
    return np.concatenate([tril] * 3, axis=1), np.concatenate([masks.transpose(0, 2, 1)] * 2, axis=1)


def _rope_expand_matrix():
    half = ROT_DIM // 2
    e = np.zeros((2 * half, 2 * LANES), np.float32)
    for lane in range(LANES):
        d = lane % HEAD_DIM
        if d < half:
            e[d, lane] = 1.0
            e[half + d, LANES + lane] = -1.0
        elif d < ROT_DIM:
            e[d - half, lane] = 1.0
            e[half + d - half, LANES + lane] = 1.0
    return e


def _rope_table_kernel(pos_ref, invf_ref, expand_ref, cos_ref, sin_ref):
    batch = pos_ref.shape[0]
    lane = lax.broadcasted_iota(jnp.int32, (1, LANES), 1) % HEAD_DIM
    passthrough = jnp.where(lane >= ROT_DIM, 1.0, 0.0).astype(F32)
    expand = expand_ref[...]
    for b in range(batch):
        pos = pos_ref[b:b + 1, :].astype(F32)
        ang = invf_ref[...] * pos
        cs = jnp.concatenate([jnp.cos(ang), jnp.sin(ang)], axis=0)
        out = None
        for piece in _split3(cs):
            part = _dot_tn(piece, expand)
            out = part if out is None else out + part
        cos_ref[b] = out[:, :LANES] + passthrough
        sin_ref[b] = out[:, LANES:]


def _rope_tables(positions):
    batch, seq = positions.shape
    half = ROT_DIM // 2
    inv_freq = jnp.power(ROPE_THETA, -jnp.arange(half, dtype=F32) * (2.0 / ROT_DIM)).reshape(half, 1)
    expand = jnp.asarray(_rope_expand_matrix(), dtype=BF16)
    out_sds = jax.ShapeDtypeStruct((batch, seq, LANES), F32)
    return pl.pallas_call(
        _rope_table_kernel,
        grid=(seq // ROPE_BLOCK,),
        in_specs=[
            pl.BlockSpec((batch, ROPE_BLOCK), lambda t: (0, t)),
            pl.BlockSpec((half, 1), lambda t: (0, 0)),
            pl.BlockSpec((2 * half, 2 * LANES), lambda t: (0, 0)),
        ],
        out_specs=[
            pl.BlockSpec((batch, ROPE_BLOCK, LANES), lambda t: (0, t, 0)),
            pl.BlockSpec((batch, ROPE_BLOCK, LANES), lambda t: (0, t, 0)),
        ],
        out_shape=[out_sds, out_sds],
        name="rope_tables",
    )(positions, inv_freq, expand)


def _rms_norm(x, w):
    return x * lax.rsqrt(jnp.mean(x * x, axis=-1, keepdims=True) + NORM_EPS) * w


def _layer_kernel(sinks_ref, xn_ref, xr_ref, cos_a_ref, sin_a_ref, cos_b_ref, sin_b_ref, nw_ref, win_ref,
                  hnw_ref, wua_ref, wuh_ref, wo_ref, lbl_ref, tril_ref, masks_ref, fnw_ref, o_ref,
                  proj0_s, proj1_s, h0_s, h1_s, q0_s, q1_s, kb0_s, kb1_s, vb0_s, vb1_s, st_s, g_s, a_s, b_s,
                  *, layer, final, blocks_per_seq):
    j = pl.program_id(0)

    @pl.when(j == 0)
    def _():
        for ref in (proj0_s, h1_s, q0_s, kb0_s, kb1_s, vb0_s, vb1_s, st_s):
            ref[...] = jnp.zeros(ref.shape, ref.dtype)

    phase = functools.partial(
        _layer_phase, sinks_ref, xn_ref, xr_ref, nw_ref, win_ref, hnw_ref, wua_ref, wuh_ref, wo_ref,
        lbl_ref, tril_ref, masks_ref, fnw_ref, o_ref, st_s, g_s, a_s, b_s, layer=layer, final=final)
    even = dict(proj=proj0_s, h=h0_s, q=q0_s, kb=kb0_s, vb=vb0_s)
    odd = dict(proj=proj1_s, h=h1_s, q=q1_s, kb=kb1_s, vb=vb1_s)
    t_a = (2 * j - 2 + blocks_per_seq) % blocks_per_seq
    t_b = (2 * j - 1 + blocks_per_seq) % blocks_per_seq
    phase(cos_a_ref, sin_a_ref, odd, even, t_idx=t_a, half=0)
    phase(cos_b_ref, sin_b_ref, even, odd, t_idx=t_b, half=1)


def _layer_phase(sinks_ref, xn_ref, xr_ref, nw_ref, win_ref, hnw_ref, wua_ref, wuh_ref, wo_ref,
                 lbl_ref, tril_ref, masks_ref, fnw_ref, o_ref, st_s, g_s, a_s, b_s, cos_ref, sin_ref, new, old,
                 *, t_idx, half, layer, final):
    tb = cos_ref.shape[1]
    win0 = half * tb
    proj_s, q_s, kb_s, vb_s = old["proj"], old["q"], old["kb"], old["vb"]

    kb_s[0:WINDOW, :] = new["kb"][tb:tb + WINDOW, :]
    vb_s[0:WINDOW, :] = new["vb"][tb:tb + WINDOW, :]

    pieces = iter(range(0, IN_WIDTH, PROJ_PIECE))

    def project(n_pieces):
        for _ in range(n_pieces):
            c0 = next(pieces, None)
            if c0 is not None:
                new["proj"][:, c0:c0 + PROJ_PIECE] = _dot_nn(new["h"][...], win_ref[:, c0:c0 + PROJ_PIECE])

    def rotate_new():
        half = ROT_DIM // 2
        lane = lax.broadcasted_iota(jnp.int32, (1, LANES), 1) % HEAD_DIM
        first_half = lane < half
        cos_t = cos_ref[0]
        sin_t = sin_ref[0]

        def rope(tile):
            partner = jnp.where(first_half, pltpu.roll(tile, LANES - half, axis=1),
                                pltpu.roll(tile, half, axis=1))
            return tile * cos_t + partner * sin_t

        scale = HEAD_DIM ** -0.5 * LOG2_E
        for j in range(ATTN_WIDTH // LANES):
            new["q"][:, j * LANES:(j + 1) * LANES] = (
                rope(new["proj"][:, OFF_Q + j * LANES:OFF_Q + (j + 1) * LANES]) * scale).astype(BF16)
        for j in range(KV_WIDTH // LANES):
            new["kb"][WINDOW:, j * LANES:(j + 1) * LANES] = rope(
                new["proj"][:, OFF_K + j * LANES:OFF_K + (j + 1) * LANES]).astype(BF16)
        new["vb"][WINDOW:, :] = new["proj"][:, OFF_V:OFF_V + KV_WIDTH].astype(BF16)

    def normalise_next():
        old["h"][...] = _rms_norm(xn_ref[0, win0:win0 + tb, :], nw_ref[...]).astype(BF16)

    row = lax.broadcasted_iota(jnp.int32, (2 * WINDOW, 2 * WINDOW), 0) % WINDOW
    col = lax.broadcasted_iota(jnp.int32, (2 * WINDOW, 2 * WINDOW), 1)
    band = (col > row) & (col <= row + WINDOW)
    upper_rows = lax.broadcasted_iota(jnp.int32, (2 * WINDOW, 1), 0) < WINDOW
    group = N_HEADS // N_KV_HEADS
    pairs_per_block = N_HEADS // 2
    n_att = (tb // WINDOW) * pairs_per_block
    att = [dict() for _ in range(n_att)]

    def att_scores(k):
        i, pr = divmod(k, pairs_per_block)
        heads = (2 * pr, 2 * pr + 1)
        kv = heads[0] // group
        q2 = jnp.concatenate(
            [q_s[i * WINDOW:(i + 1) * WINDOW, h * HEAD_DIM:(h + 1) * HEAD_DIM] for h in heads], axis=0)
        kh = kb_s[i * WINDOW:(i + 2) * WINDOW, kv * HEAD_DIM:(kv + 1) * HEAD_DIM]
        has_prev = jnp.logical_or(t_idx > 0, i > 0)
        mask = band & jnp.logical_or(col >= WINDOW, has_prev)
        att[k]["s"] = jnp.where(mask, _dot_nt(q2, kh), MASK_VALUE)
        att[k]["sink"] = jnp.where(upper_rows, sinks_ref[heads[0]], sinks_ref[heads[1]]) * LOG2_E

    def att_max(k):
        att[k]["m"] = jnp.maximum(jnp.max(att[k]["s"], axis=-1, keepdims=True), att[k]["sink"])

    def att_probs(k):
        m = att[k]["m"]
        p = jnp.exp2(att[k].pop("s") - m)
        att[k]["rdenom"] = 1.0 / (jnp.sum(p, axis=-1, keepdims=True) + jnp.exp2(att[k].pop("sink") - m))
        att[k]["p"] = p.astype(BF16)

    def att_out(k):
        i, pr = divmod(k, pairs_per_block)
        kv = (2 * pr) // group
        vh = vb_s[i * WINDOW:(i + 2) * WINDOW, kv * HEAD_DIM:(kv + 1) * HEAD_DIM]
        o = _dot_nn(att[k].pop("p"), vh) * att[k].pop("rdenom")
        o = jnp.concatenate([o[:WINDOW], o[WINDOW:]], axis=1)
        z = proj_s[i * WINDOW:(i + 1) * WINDOW, OFF_Z + pr * LANES:OFF_Z + (pr + 1) * LANES]
        a_s[i * WINDOW:(i + 1) * WINDOW, pr * LANES:(pr + 1) * LANES] = (o * (z * _sigmoid(z))).astype(a_s.dtype)

    att_stages = (att_scores, att_max, att_probs, att_out)

    lbl = lbl_ref[...]
    lb_e = jnp.exp(lbl - jnp.max(lbl, axis=0, keepdims=True))
    lb_soft = lb_e / jnp.sum(lb_e, axis=0, keepdims=True)
    lb = jnp.sum(lb_soft[0:layer + 1, :], axis=0, keepdims=True) - lb_soft[0:1, :]
    one_minus_lb = 1.0 - lb
    hnw = hnw_ref[...]
    carry_state = jnp.where(t_idx == 0, 0.0, 1.0).astype(F32)

    n_lvl = len(LEVELS)
    pair_width = 2 * HGRN_HEAD_DIM
    n_chunks = tb // CHUNK
    n_pairs = HGRN_HEADS // 2
    hg = [dict() for _ in range(n_chunks)]

    def hgrn_gates(c):
        rows = slice(c * CHUNK, (c + 1) * CHUNK)
        fx = proj_s[rows, OFF_HF:OFF_HF + HGRN_WIDTH]
        e = jnp.exp(-jnp.abs(fx))
        r = 1.0 / (1.0 + e)
        er = e * r
        pos_f = fx >= 0
        f = lb + one_minus_lb * jnp.where(pos_f, r, er)
        hg[c]["f"] = f
        hg[c]["k"] = one_minus_lb * jnp.where(pos_f, er, r)
        qx = proj_s[rows, OFF_HQ:OFF_HQ + HGRN_WIDTH]
        hg[c]["q"] = qx * _sigmoid(qx)
        logf3 = jnp.concatenate(_split3(jnp.log(f) * LOG2_E), axis=0)
        g_s[c] = _dot_nn(tril_ref[...], logf3)

    def hgrn_decay(c):
        def ref_rows(first, step):
            return jnp.concatenate(
                [jnp.broadcast_to(g_s[c, first + step * i:first + step * i + 1, :], (8, HGRN_WIDTH))
                 for i in range(CHUNK // 8)], axis=0)

        g = g_s[c]
        factors = []
        for h in LEVELS:
            if h >= 4:
                d = jnp.concatenate(
                    [g_s[c, b * 2 * h:(b + 1) * 2 * h, :] - g_s[c, b * 2 * h + h - 1:b * 2 * h + h, :]
                     for b in range(CHUNK // (2 * h))], axis=0)
                factors.append(jnp.exp2(-jnp.abs(d)))
            elif h == 2:
                low = lax.broadcasted_iota(jnp.int32, (CHUNK, 1), 0) % 8 < 4
                factors.append(jnp.exp2(-jnp.abs(g - jnp.where(low, ref_rows(1, 8), ref_rows(5, 8)))))
            else:
                odd = lax.broadcasted_iota(jnp.int32, (CHUNK, 1), 0) % 2 == 1
                factors.append(jnp.where(odd, hg[c].pop("f"), 1.0))
        hg[c]["levels"] = factors
        hg[c]["g_cum"] = jnp.exp2(g)
        hg[c]["g_rest"] = jnp.exp2(g_s[c, CHUNK - 1:CHUNK, :] - g)

    def hgrn_scores(c):
        levels = hg[c].pop("levels")
        hg[c]["scores"] = []
        for pair in range(n_pairs):
            ps = slice(pair * pair_width, (pair + 1) * pair_width)
            q2 = hg[c]["q"][:, ps]
            k2 = hg[c]["k"][:, ps]
            scores = masks_ref[n_lvl] * _dot_nt(_block_diag2(k2.astype(BF16)), q2.astype(BF16))
            for li in range(n_lvl):
                d_l = levels[li][:, ps]
                scores = scores + masks_ref[li] * _dot_nt(
                    _block_diag2((k2 * d_l).astype(BF16)), (q2 * d_l).astype(BF16))
            hg[c]["scores"].append(scores.astype(BF16))

    def hgrn_out(c):
        rows = slice(c * CHUNK, (c + 1) * CHUNK)
        qin = hg[c].pop("q")
        kin = hg[c].pop("k")
        gx = proj_s[rows, OFF_HG:OFF_HG + HGRN_WIDTH]
        gate = gx * _sigmoid(gx)
        for pair in range(n_pairs):
            ps = slice(pair * pair_width, (pair + 1) * pair_width)
            v2 = proj_s[rows, OFF_HI + pair * pair_width:OFF_HI + (pair + 1) * pair_width].astype(BF16)
            o2 = _dot_tn(hg[c]["scores"][pair], _block_diag2(v2))
            g_cum = hg[c]["g_cum"][:, ps]
            g_rest = hg[c]["g_rest"][:, ps]
            qg = (qin[:, ps] * g_cum).astype(BF16)
            kg = (kin[:, ps] * g_rest).astype(BF16)
            for j in range(2):
                head = 2 * pair + j
                ls = slice(j * HGRN_HEAD_DIM, (j + 1) * HGRN_HEAD_DIM)
                hs = slice(head * HGRN_HEAD_DIM, (head + 1) * HGRN_HEAD_DIM)
                state = st_s[head]
                if c == 0:
                    state = state * carry_state
                o = o2[:, ls] + _dot_nt(qg[:, ls], state.astype(BF16))
                st_s[head] = state * g_cum[CHUNK - 1:CHUNK, ls] + _dot_tn(v2[:, ls], kg[:, ls])
                o = o * lax.rsqrt(jnp.mean(o * o, axis=-1, keepdims=True) + NORM_EPS) * hnw
                b_s[rows, hs] = (o * gate[:, hs]).astype(b_s.dtype)
        hg[c].clear()

    hgrn_stages = (hgrn_gates, hgrn_decay, hgrn_scores, hgrn_out)

    def merge(half_idx):
        rows = slice(half_idx * (tb // 2), (half_idx + 1) * (tb // 2))
        win_rows = slice(win0 + rows.start, win0 + rows.stop)
        up_a = _dot_nn(a_s[rows, :], wua_ref[...])
        up_h = _dot_nn(b_s[rows, :], wuh_ref[...])
        merged = (_sigmoid(proj_s[rows, OFF_GA:OFF_GA + D_MODEL]) * up_a
                  + _sigmoid(proj_s[rows, OFF_GH:OFF_GH + D_MODEL]) * up_h)
        y = xr_ref[0, win_rows, :] + _dot_nn(merged.astype(BF16), wo_ref[...])
        if final:
            y = _rms_norm(y, fnw_ref[...])
        o_ref[0, win_rows, :] = y

    n_slots = max(n_att + len(att_stages) - 1, 2 * (n_chunks - 1) + len(hgrn_stages))
    loop_pieces = IN_WIDTH // PROJ_PIECE - TAIL_PIECES
    pieces_per_slot = -(-loop_pieces // n_slots)
    rotate_slot = -(-(OFF_Z // PROJ_PIECE) // pieces_per_slot)
    att_half = -(-(tb // 2) // WINDOW) * pairs_per_block
    merge_slot = max(att_half + len(att_stages) - 1, 2 * (n_chunks // 2 - 1) + len(hgrn_stages))
    for slot in range(n_slots):
        slot_pieces = max(0, min(pieces_per_slot, loop_pieces - slot * pieces_per_slot))
        project(slot_pieces // 2)
        for s, stage in enumerate(att_stages):
            if 0 <= slot - s < n_att:
                stage(slot - s)
        project(slot_pieces - slot_pieces // 2)
        for s, stage in enumerate(hgrn_stages):
            if (slot - s) % 2 == 0 and 0 <= (slot - s) // 2 < n_chunks:
                stage((slot - s) // 2)
        if slot == rotate_slot:
            rotate_new()
        if slot == rotate_slot + 2:
            normalise_next()
        if slot == merge_slot:
            merge(0)
    project(IN_WIDTH // PROJ_PIECE)
    if merge_slot >= n_slots:
        merge(0)
    merge(1)


def _layer_call(x, cos_tab, sin_tab, sinks, norm_w, w_in, hgrn_norm_w, w_up_attn, w_up_hgrn, w_out,
                lb_logits, sums, masks, final_norm_w, *, layer, final):
    batch, seq, d = x.shape
    tb = TOKEN_BLOCK
    n_blocks = batch * seq // tb
    once = pl.Buffered(1)
    const2 = lambda g: (0, 0)
    const3 = lambda g: (0, 0, 0)
    n_steps = n_blocks // 2 + 1
    pair_next = lambda j: (jnp.minimum(j, n_blocks // 2 - 1), 0, 0)
    pair_old = lambda j: (jnp.maximum(j - 1, 0), 0, 0)
    blk_a = lambda j: (jnp.maximum(2 * j - 1, 0), 0, 0)
    blk_b = lambda j: (jnp.minimum(2 * j, n_blocks - 1), 0, 0)
    x_pairs = x.reshape(n_blocks // 2, 2 * tb, d)
    cos_blocks = cos_tab.reshape(n_blocks, tb, LANES)
    sin_blocks = sin_tab.reshape(n_blocks, tb, LANES)
    out = pl.pallas_call(
        functools.partial(_layer_kernel, layer=layer, final=final, blocks_per_seq=seq // tb),
        grid=(n_steps,),
        in_specs=[
            pl.BlockSpec(memory_space=pltpu.SMEM),
            pl.BlockSpec((1, 2 * tb, d), pair_next),
            pl.BlockSpec((1, 2 * tb, d), pair_old),
            pl.BlockSpec((1, tb, LANES), blk_a),
            pl.BlockSpec((1, tb, LANES), blk_a),
            pl.BlockSpec((1, tb, LANES), blk_b),
            pl.BlockSpec((1, tb, LANES), blk_b),
            pl.BlockSpec((1, d), const2, pipeline_mode=once),
            pl.BlockSpec((d, IN_WIDTH), const2, pipeline_mode=once),
            pl.BlockSpec((1, HGRN_HEAD_DIM), const2, pipeline_mode=once),
            pl.BlockSpec((ATTN_WIDTH, d), const2, pipeline_mode=once),
            pl.BlockSpec((HGRN_WIDTH, d), const2, pipeline_mode=once),
            pl.BlockSpec((d, d), const2, pipeline_mode=once),
            pl.BlockSpec((DEPTH, HGRN_WIDTH), const2, pipeline_mode=once),
            pl.BlockSpec((CHUNK, 3 * CHUNK), const2, pipeline_mode=once),
            pl.BlockSpec((len(LEVELS) + 1, 2 * CHUNK, CHUNK), const3, pipeline_mode=once),
            pl.BlockSpec((1, d), const2, pipeline_mode=once),
        ],
        out_specs=pl.BlockSpec((1, 2 * tb, d), pair_old),
        out_shape=jax.ShapeDtypeStruct((n_blocks // 2, 2 * tb, d), x.dtype),
        scratch_shapes=[
            pltpu.VMEM((tb, IN_WIDTH), F32),
            pltpu.VMEM((tb, IN_WIDTH), F32),
            pltpu.VMEM((tb, d), BF16),
            pltpu.VMEM((tb, d), BF16),
            pltpu.VMEM((tb, ATTN_WIDTH), BF16),
            pltpu.VMEM((tb, ATTN_WIDTH), BF16),
            pltpu.VMEM((WINDOW + tb, KV_WIDTH), BF16),
            pltpu.VMEM((WINDOW + tb, KV_WIDTH), BF16),
            pltpu.VMEM((WINDOW + tb, KV_WIDTH), BF16),
            pltpu.VMEM((WINDOW + tb, KV_WIDTH), BF16),
            pltpu.VMEM((HGRN_HEADS, HGRN_HEAD_DIM, HGRN_HEAD_DIM), F32),
            pltpu.VMEM((tb // CHUNK, CHUNK, HGRN_WIDTH), F32),
            pltpu.VMEM((tb, ATTN_WIDTH), BF16),
            pltpu.VMEM((tb, HGRN_WIDTH), BF16),
        ],
        compiler_params=pltpu.CompilerParams(
            dimension_semantics=("arbitrary",),
            vmem_limit_bytes=VMEM_LIMIT_BYTES),
        name=f"hybrid_layer_{layer}",
    )(sinks, x_pairs, x_pairs, cos_blocks, sin_blocks, cos_blocks, sin_blocks, norm_w, w_in, hgrn_norm_w,
      w_up_attn, w_up_hgrn, w_out, lb_logits, sums, masks, final_norm_w)
    return out.reshape(batch, seq, d)


def kernel(x, positions, norm_w, w_in, attn_sinks, hgrn_norm_w, w_up_attn, w_up_hgrn, w_out, lb_logits,
           final_norm_w):
    depth = w_in.shape[0]
    assert depth == DEPTH and x.shape[1] % (2 * TOKEN_BLOCK) == 0 and x.shape[1] % ROPE_BLOCK == 0
    cos_tab, sin_tab = _rope_tables(positions)
    sums_np, masks_np = _hgrn_constants()
    sums = jnp.asarray(sums_np, dtype=BF16)
    masks = jnp.asarray(masks_np, dtype=F32)
    fnw = final_norm_w.reshape(1, D_MODEL)
    for layer in range(depth):
        x = _layer_call(
            x, cos_tab, sin_tab, attn_sinks[layer], norm_w[layer].reshape(1, D_MODEL),
            w_in[layer].astype(BF16), hgrn_norm_w[layer].reshape(1, HGRN_HEAD_DIM),
            w_up_attn[layer].astype(BF16), w_up_hgrn[layer].astype(BF16), w_out[layer].astype(BF16),
            lb_logits, sums, masks, fnw, layer=layer, final=(layer == depth - 1))
    return x
```

```python
import functools

import numpy as np
import jax
import jax.numpy as jnp
from jax import lax
from jax.experimental import pallas as pl
from jax.experimental.pallas import tpu as pltpu

D_MODEL = 1024
DEPTH = 2
N_HEADS = 8
N_KV_HEADS = 2
HEAD_DIM = 64
ATTN_WIDTH = N_HEADS * HEAD_DIM
KV_WIDTH = N_KV_HEADS * HEAD_DIM
WINDOW = 128
ROT_DIM = HEAD_DIM // 4
ROPE_THETA = 500000.0
HGRN_HEADS = 4
HGRN_HEAD_DIM = 128
HGRN_WIDTH = HGRN_HEADS * HGRN_HEAD_DIM
CHUNK = 64
NORM_EPS = 1e-6
MASK_VALUE = -1e30
LOG2_E = 1.4426950408889634
IN_WIDTH = 2 * ATTN_WIDTH + 2 * KV_WIDTH + 4 * HGRN_WIDTH + 2 * D_MODEL

OFF_Q = 0
OFF_K = OFF_Q + ATTN_WIDTH
OFF_V = OFF_K + KV_WIDTH
OFF_Z = OFF_V + KV_WIDTH
OFF_HQ = OFF_Z + ATTN_WIDTH
OFF_HF = OFF_HQ + HGRN_WIDTH
OFF_HI = OFF_HF + HGRN_WIDTH
OFF_HG = OFF_HI + HGRN_WIDTH
OFF_GA = OFF_HG + HGRN_WIDTH
OFF_GH = OFF_GA + D_MODEL

LANES = 128
TOKEN_BLOCK = 256
ROPE_BLOCK = 512
PROJ_PIECE = 256
TAIL_PIECES = 5
VMEM_LIMIT_BYTES = 56 * 1024 * 1024

LEVELS = (32, 16, 8, 4, 2, 1)

F32 = jnp.float32
BF16 = jnp.bfloat16


def _dot_nn(a, b):
    return lax.dot_general(a, b, (((1,), (0,)), ((), ())), preferred_element_type=F32)


def _dot_nt(a, b):
    return lax.dot_general(a, b, (((1,), (1,)), ((), ())), preferred_element_type=F32)


def _dot_tn(a, b):
    return lax.dot_general(a, b, (((0,), (0,)), ((), ())), preferred_element_type=F32)


def _split3(x):
    hi = x.astype(BF16)
    r1 = x - hi.astype(F32)
    mid = r1.astype(BF16)
    lo = (r1 - mid.astype(F32)).astype(BF16)
    return hi, mid, lo


def _sigmoid(x):
    return 0.5 * jnp.tanh(0.5 * x) + 0.5


def _block_diag2(a):
    zero = jnp.zeros((a.shape[0], LANES), a.dtype)
    top = jnp.concatenate([a[:, :LANES], zero], axis=1)
    bottom = jnp.concatenate([zero, a[:, LANES:]], axis=1)
    return jnp.concatenate([top, bottom], axis=0)


def _hgrn_constants():
    t = np.arange(CHUNK)[:, None]
    u = np.arange(CHUNK)[None, :]
    masks = []
    for h in LEVELS:
        right = (t // h) % 2 == 1
        masks.append((t // (2 * h) == u // (2 * h)) & right & ((u // h) % 2 == 0))
    masks.append(t == u)
    masks = np.stack(masks, axis=0).astype(np.float32)
    tril = (u <= t).astype(np.float32)
    return np.concatenate([tril] * 3, axis=1), np.concatenate([masks] * 2, axis=2)


def _rope_expand_matrix():
    half = ROT_DIM // 2
    e = np.zeros((2 * half, 2 * LANES), np.float32)
    for lane in range(LANES):
        d = lane % HEAD_DIM
        if d < half:
            e[d, lane] = 1.0
            e[half + d, LANES + lane] = -1.0
        elif d < ROT_DIM:
            e[d - half, lane] = 1.0
            e[half + d - half, LANES + lane] = 1.0
    return e


def _rope_table_kernel(pos_ref, invf_ref, expand_ref, cos_ref, sin_ref):
    batch = pos_ref.shape[0]
    lane = lax.broadcasted_iota(jnp.int32, (1, LANES), 1) % HEAD_DIM
    passthrough = jnp.where(lane >= ROT_DIM, 1.0, 0.0).astype(F32)
    expand = expand_ref[...]
    for b in range(batch):
        pos = pos_ref[b:b + 1, :].astype(F32)
        ang = invf_ref[...] * pos
        cs = jnp.concatenate([jnp.cos(ang), jnp.sin(ang)], axis=0)
        out = None
        for piece in _split3(cs):
            part = _dot_tn(piece, expand)
            out = part if out is None else out + part
        cos_ref[b] = out[:, :LANES] + passthrough
        sin_ref[b] = out[:, LANES:]


def _rope_tables(positions):
    batch, seq = positions.shape
    half = ROT_DIM // 2
    inv_freq = jnp.power(ROPE_THETA, -jnp.arange(half, dtype=F32) * (2.0 / ROT_DIM)).reshape(half, 1)
    expand = jnp.asarray(_rope_expand_matrix(), dtype=BF16)
    out_sds = jax.ShapeDtypeStruct((batch, seq, LANES), F32)
    return pl.pallas_call(
        _rope_table_kernel,
        grid=(seq // ROPE_BLOCK,),
        in_specs=[
            pl.BlockSpec((batch, ROPE_BLOCK), lambda t: (0, t)),
            pl.BlockSpec((half, 1), lambda t: (0, 0)),
            pl.BlockSpec((2 * half, 2 * LANES), lambda t: (0, 0)),
        ],
        out_specs=[
            pl.BlockSpec((batch, ROPE_BLOCK, LANES), lambda t: (0, t, 0)),
            pl.BlockSpec((batch, ROPE_BLOCK, LANES), lambda t: (0, t, 0)),
        ],
        out_shape=[out_sds, out_sds],
        name="rope_tables",
    )(positions, inv_freq, expand)


def _rms_norm(x, w):
    return x * lax.rsqrt(jnp.mean(x * x, axis=-1, keepdims=True) + NORM_EPS) * w


def _layer_kernel(sinks_ref, xn_ref, xr_ref, cos_a_ref, sin_a_ref, cos_b_ref, sin_b_ref, nw_ref, win_ref,
                  hnw_ref, wua_ref, wuh_ref, wo_ref, lbl_ref, tril_ref, masks_ref, fnw_ref, o_ref,
                  proj0_s, proj1_s, h0_s, h1_s, q0_s, q1_s, kb0_s, kb1_s, vb0_s, vb1_s, st_s, g_s, a_s, b_s,
                  *, layer, final, blocks_per_seq):
    j = pl.program_id(0)

    @pl.when(j == 0)
    def _():
        for ref in (proj0_s, h1_s, q0_s, kb0_s, kb1_s, vb0_s, vb1_s, st_s):
            ref[...] = jnp.zeros(ref.shape, ref.dtype)

    phase = functools.partial(
        _layer_phase, sinks_ref, xn_ref, xr_ref, nw_ref, win_ref, hnw_ref, wua_ref, wuh_ref, wo_ref,
        lbl_ref, tril_ref, masks_ref, fnw_ref, o_ref, st_s, g_s, a_s, b_s, layer=layer, final=final)
    even = dict(proj=proj0_s, h=h0_s, q=q0_s, kb=kb0_s, vb=vb0_s)
    odd = dict(proj=proj1_s, h=h1_s, q=q1_s, kb=kb1_s, vb=vb1_s)
    t_a = (2 * j - 2 + blocks_per_seq) % blocks_per_seq
    t_b = (2 * j - 1 + blocks_per_seq) % blocks_per_seq
    phase(cos_a_ref, sin_a_ref, odd, even, t_idx=t_a, half=0)
    phase(cos_b_ref, sin_b_ref, even, odd, t_idx=t_b, half=1)


def _layer_phase(sinks_ref, xn_ref, xr_ref, nw_ref, win_ref, hnw_ref, wua_ref, wuh_ref, wo_ref,
                 lbl_ref, tril_ref, masks_ref, fnw_ref, o_ref, st_s, g_s, a_s, b_s, cos_ref, sin_ref, new, old,
                 *, t_idx, half, layer, final):
    tb = cos_ref.shape[1]
    win0 = half * tb
    proj_s, q_s, kb_s, vb_s = old["proj"], old["q"], old["kb"], old["vb"]

    kb_s[0:WINDOW, :] = new["kb"][tb:tb + WINDOW, :]
    vb_s[0:WINDOW, :] = new["vb"][tb:tb + WINDOW, :]

    pieces = iter(range(0, IN_WIDTH, PROJ_PIECE))

    def project(n_pieces):
        for _ in range(n_pieces):
            c0 = next(pieces, None)
            if c0 is not None:
                new["proj"][:, c0:c0 + PROJ_PIECE] = _dot_nn(new["h"][...], win_ref[:, c0:c0 + PROJ_PIECE])

    def rotate_new():
        half = ROT_DIM // 2
        lane = lax.broadcasted_iota(jnp.int32, (1, LANES), 1) % HEAD_DIM
        first_half = lane < half
        cos_t = cos_ref[0]
        sin_t = sin_ref[0]

        def rope(tile):
            partner = jnp.where(first_half, pltpu.roll(tile, LANES - half, axis=1),
                                pltpu.roll(tile, half, axis=1))
            return tile * cos_t + partner * sin_t

        scale = HEAD_DIM ** -0.5 * LOG2_E
        for j in range(ATTN_WIDTH // LANES):
            new["q"][:, j * LANES:(j + 1) * LANES] = (
                rope(new["proj"][:, OFF_Q + j * LANES:OFF_Q + (j + 1) * LANES]) * scale).astype(BF16)
        for j in range(KV_WIDTH // LANES):
            new["kb"][WINDOW:, j * LANES:(j + 1) * LANES] = rope(
                new["proj"][:, OFF_K + j * LANES:OFF_K + (j + 1) * LANES]).astype(BF16)
        new["vb"][WINDOW:, :] = new["proj"][:, OFF_V:OFF_V + KV_WIDTH].astype(BF16)

    def normalise_next():
        old["h"][...] = _rms_norm(xn_ref[0, win0:win0 + tb, :], nw_ref[...]).astype(BF16)

    row = lax.broadcasted_iota(jnp.int32, (2 * WINDOW, 2 * WINDOW), 0) % WINDOW
    col = lax.broadcasted_iota(jnp.int32, (2 * WINDOW, 2 * WINDOW), 1)
    band = (col > row) & (col <= row + WINDOW)
    upper_rows = lax.broadcasted_iota(jnp.int32, (2 * WINDOW, 1), 0) < WINDOW
    group = N_HEADS // N_KV_HEADS
    pairs_per_block = N_HEADS // 2
    n_att = (tb // WINDOW) * pairs_per_block
    att = [dict() for _ in range(n_att)]

    def att_scores(k):
        i, pr = divmod(k, pairs_per_block)
        heads = (2 * pr, 2 * pr + 1)
        kv = heads[0] // group
        q2 = jnp.concatenate(
            [q_s[i * WINDOW:(i + 1) * WINDOW, h * HEAD_DIM:(h + 1) * HEAD_DIM] for h in heads], axis=0)
        kh = kb_s[i * WINDOW:(i + 2) * WINDOW, kv * HEAD_DIM:(kv + 1) * HEAD_DIM]
        has_prev = jnp.logical_or(t_idx > 0, i > 0)
        mask = band & jnp.logical_or(col >= WINDOW, has_prev)
        att[k]["s"] = jnp.where(mask, _dot_nt(q2, kh), MASK_VALUE)
        att[k]["sink"] = jnp.where(upper_rows, sinks_ref[heads[0]], sinks_ref[heads[1]]) * LOG2_E

    def att_max(k):
        att[k]["m"] = jnp.maximum(jnp.max(att[k]["s"], axis=-1, keepdims=True), att[k]["sink"])

    def att_probs(k):
        m = att[k]["m"]
        p = jnp.exp2(att[k].pop("s") - m)
        att[k]["rdenom"] = 1.0 / (jnp.sum(p, axis=-1, keepdims=True) + jnp.exp2(att[k].pop("sink") - m))
        att[k]["p"] = p.astype(BF16)

    def att_out(k):
        i, pr = divmod(k, pairs_per_block)
        kv = (2 * pr) // group
        vh = vb_s[i * WINDOW:(i + 2) * WINDOW, kv * HEAD_DIM:(kv + 1) * HEAD_DIM]
        o = _dot_nn(att[k].pop("p"), vh) * att[k].pop("rdenom")
        o = jnp.concatenate([o[:WINDOW], o[WINDOW:]], axis=1)
        z = proj_s[i * WINDOW:(i + 1) * WINDOW, OFF_Z + pr * LANES:OFF_Z + (pr + 1) * LANES]
        a_s[i * WINDOW:(i + 1) * WINDOW, pr * LANES:(pr + 1) * LANES] = (o * (z * _sigmoid(z))).astype(a_s.dtype)

    att_stages = (att_scores, att_max, att_probs, att_out)

    lbl = lbl_ref[...]
    lb_e = jnp.exp(lbl - jnp.max(lbl, axis=0, keepdims=True))
    lb_soft = lb_e / jnp.sum(lb_e, axis=0, keepdims=True)
    lb = jnp.sum(lb_soft[0:layer + 1, :], axis=0, keepdims=True) - lb_soft[0:1, :]
    one_minus_lb = 1.0 - lb
    hnw = hnw_ref[...]
    carry_state = jnp.where(t_idx == 0, 0.0, 1.0).astype(F32)

    n_lvl = len(LEVELS)
    pair_width = 2 * HGRN_HEAD_DIM
    n_chunks = tb // CHUNK
    n_pairs = HGRN_HEADS // 2
    hg = [dict() for _ in range(n_chunks)]

    def hgrn_gates(c):
        rows = slice(c * CHUNK, (c + 1) * CHUNK)
        fx = proj_s[rows, OFF_HF:OFF_HF + HGRN_WIDTH]
        e = jnp.exp(-jnp.abs(fx))
        r = 1.0 / (1.0 + e)
        er = e * r
        pos_f = fx >= 0
        f = lb + one_minus_lb * jnp.where(pos_f, r, er)
        hg[c]["f"] = f
        hg[c]["k"] = one_minus_lb * jnp.where(pos_f, er, r)
        qx = proj_s[rows, OFF_HQ:OFF_HQ + HGRN_WIDTH]
        hg[c]["q"] = qx * _sigmoid(qx)
        logf3 = jnp.concatenate(_split3(jnp.log(f) * LOG2_E), axis=0)
        g_s[c] = _dot_nn(tril_ref[...], logf3)

    def hgrn_decay(c):
        def ref_rows(first, step):
            return jnp.concatenate(
                [jnp.broadcast_to(g_s[c, first + step * i:first + step * i + 1, :], (8, HGRN_WIDTH))
                 for i in range(CHUNK // 8)], axis=0)

        g = g_s[c]
        factors = []
        for h in LEVELS:
            if h >= 4:
                d = jnp.concatenate(
                    [g_s[c, b * 2 * h:(b + 1) * 2 * h, :] - g_s[c, b * 2 * h + h - 1:b * 2 * h + h, :]
                     for b in range(CHUNK // (2 * h))], axis=0)
                factors.append(jnp.exp2(-jnp.abs(d)))
            elif h == 2:
                low = lax.broadcasted_iota(jnp.int32, (CHUNK, 1), 0) % 8 < 4
                factors.append(jnp.exp2(-jnp.abs(g - jnp.where(low, ref_rows(1, 8), ref_rows(5, 8)))))
            else:
                odd = lax.broadcasted_iota(jnp.int32, (CHUNK, 1), 0) % 2 == 1
                factors.append(jnp.where(odd, hg[c].pop("f"), 1.0))
        hg[c]["levels"] = factors
        hg[c]["g_cum"] = jnp.exp2(g)
        hg[c]["g_rest"] = jnp.exp2(g_s[c, CHUNK - 1:CHUNK, :] - g)

    def hgrn_scores(c):
        levels = hg[c].pop("levels")
        hg[c]["scores"] = []
        for pair in range(n_pairs):
            ps = slice(pair * pair_width, (pair + 1) * pair_width)
            q2 = hg[c]["q"][:, ps]
            k2 = hg[c]["k"][:, ps]
            scores = masks_ref[n_lvl] * _dot_nt(q2.astype(BF16), _block_diag2(k2.astype(BF16)))
            for li in range(n_lvl):
                d_l = levels[li][:, ps]
                scores = scores + masks_ref[li] * _dot_nt(
                    (q2 * d_l).astype(BF16), _block_diag2((k2 * d_l).astype(BF16)))
            hg[c]["scores"].append(scores.astype(BF16))

    def hgrn_out(c):
        rows = slice(c * CHUNK, (c + 1) * CHUNK)
        qin = hg[c].pop("q")
        kin = hg[c].pop("k")
        gx = proj_s[rows, OFF_HG:OFF_HG + HGRN_WIDTH]
        gate = gx * _sigmoid(gx)
        for pair in range(n_pairs):
            ps = slice(pair * pair_width, (pair + 1) * pair_width)
            v2 = proj_s[rows, OFF_HI + pair * pair_width:OFF_HI + (pair + 1) * pair_width].astype(BF16)
            o2 = _dot_nn(hg[c]["scores"][pair], _block_diag2(v2))
            g_cum = hg[c]["g_cum"][:, ps]
            g_rest = hg[c]["g_rest"][:, ps]
            qg = (qin[:, ps] * g_cum).astype(BF16)
            kg = (kin[:, ps] * g_rest).astype(BF16)
            for j in range(2):
                head = 2 * pair + j
                ls = slice(j * HGRN_HEAD_DIM, (j + 1) * HGRN_HEAD_DIM)
                hs = slice(head * HGRN_HEAD_DIM, (head + 1) * HGRN_HEAD_DIM)
                state = st_s[head]
                if c == 0:
                    state = state * carry_state
                o = o2[:, ls] + _dot_nt(qg[:, ls], state.astype(BF16))
                st_s[head] = state * g_cum[CHUNK - 1:CHUNK, ls] + _dot_tn(v2[:, ls], kg[:, ls])
                o = o * lax.rsqrt(jnp.mean(o * o, axis=-1, keepdims=True) + NORM_EPS) * hnw
                b_s[rows, hs] = (o * gate[:, hs]).astype(b_s.dtype)
        hg[c].clear()

    hgrn_stages = (hgrn_gates, hgrn_decay, hgrn_scores, hgrn_out)

    def merge(half_idx):
        rows = slice(half_idx * (tb // 2), (half_idx + 1) * (tb // 2))
        win_rows = slice(win0 + rows.start, win0 + rows.stop)
        up_a = _dot_nn(a_s[rows, :], wua_ref[...])
        up_h = _dot_nn(b_s[rows, :], wuh_ref[...])
        merged = (_sigmoid(proj_s[rows, OFF_GA:OFF_GA + D_MODEL]) * up_a
                  + _sigmoid(proj_s[rows, OFF_GH:OFF_GH + D_MODEL]) * up_h)
        y = xr_ref[0, win_rows, :] + _dot_nn(merged.astype(BF16), wo_ref[...])
        if final:
            y = _rms_norm(y, fnw_ref[...])
        o_ref[0, win_rows, :] = y

    n_slots = max(n_att + len(att_stages) - 1, 2 * (n_chunks - 1) + len(hgrn_stages))
    loop_pieces = IN_WIDTH // PROJ_PIECE - TAIL_PIECES
    pieces_per_slot = -(-loop_pieces // n_slots)
    rotate_slot = -(-(OFF_Z // PROJ_PIECE) // pieces_per_slot)
    att_half = -(-(tb // 2) // WINDOW) * pairs_per_block
    merge_slot = max(att_half + len(att_stages) - 1, 2 * (n_chunks // 2 - 1) + len(hgrn_stages))
    for slot in range(n_slots):
        slot_pieces = max(0, min(pieces_per_slot, loop_pieces - slot * pieces_per_slot))
        project(slot_pieces // 2)
        for s, stage in enumerate(att_stages):
            if 0 <= slot - s < n_att:
                stage(slot - s)
        project(slot_pieces - slot_pieces // 2)
        for s, stage in enumerate(hgrn_stages):
            if (slot - s) % 2 == 0 and 0 <= (slot - s) // 2 < n_chunks:
                stage((slot - s) // 2)
        if slot == rotate_slot:
            rotate_new()
        if slot == rotate_slot + 2:
            normalise_next()
        if slot == merge_slot:
            merge(0)
    project(IN_WIDTH // PROJ_PIECE)
    if merge_slot >= n_slots:
        merge(0)
    merge(1)


def _layer_call(x, cos_tab, sin_tab, sinks, norm_w, w_in, hgrn_norm_w, w_up_attn, w_up_hgrn, w_out,
                lb_logits, sums, masks, final_norm_w, *, layer, final):
    batch, seq, d = x.shape
    tb = TOKEN_BLOCK
    n_blocks = batch * seq // tb
    once = pl.Buffered(1)
    const2 = lambda g: (0, 0)
    const3 = lambda g: (0, 0, 0)
    n_steps = n_blocks // 2 + 1
    pair_next = lambda j: (jnp.minimum(j, n_blocks // 2 - 1), 0, 0)
    pair_old = lambda j: (jnp.maximum(j - 1, 0), 0, 0)
    blk_a = lambda j: (jnp.maximum(2 * j - 1, 0), 0, 0)
    blk_b = lambda j: (jnp.minimum(2 * j, n_blocks - 1), 0, 0)
    x_pairs = x.reshape(n_blocks // 2, 2 * tb, d)
    cos_blocks = cos_tab.reshape(n_blocks, tb, LANES)
    sin_blocks = sin_tab.reshape(n_blocks, tb, LANES)
    out = pl.pallas_call(
        functools.partial(_layer_kernel, layer=layer, final=final, blocks_per_seq=seq // tb),
        grid=(n_steps,),
        in_specs=[
            pl.BlockSpec(memory_space=pltpu.SMEM),
            pl.BlockSpec((1, 2 * tb, d), pair_next),
            pl.BlockSpec((1, 2 * tb, d), pair_old),
            pl.BlockSpec((1, tb, LANES), blk_a),
            pl.BlockSpec((1, tb, LANES), blk_a),
            pl.BlockSpec((1, tb, LANES), blk_b),
            pl.BlockSpec((1, tb, LANES), blk_b),
            pl.BlockSpec((1, d), const2, pipeline_mode=once),
            pl.BlockSpec((d, IN_WIDTH), const2, pipeline_mode=once),
            pl.BlockSpec((1, HGRN_HEAD_DIM), const2, pipeline_mode=once),
            pl.BlockSpec((ATTN_WIDTH, d), const2, pipeline_mode=once),
            pl.BlockSpec((HGRN_WIDTH, d), const2, pipeline_mode=once),
            pl.BlockSpec((d, d), const2, pipeline_mode=once),
            pl.BlockSpec((DEPTH, HGRN_WIDTH), const2, pipeline_mode=once),
            pl.BlockSpec((CHUNK, 3 * CHUNK), const2, pipeline_mode=once),
            pl.BlockSpec((len(LEVELS) + 1, CHUNK, 2 * CHUNK), const3, pipeline_mode=once),
            pl.BlockSpec((1, d), const2, pipeline_mode=once),
        ],
        out_specs=pl.BlockSpec((1, 2 * tb, d), pair_old),
        out_shape=jax.ShapeDtypeStruct((n_blocks // 2, 2 * tb, d), x.dtype),
        scratch_shapes=[
            pltpu.VMEM((tb, IN_WIDTH), F32),
            pltpu.VMEM((tb, IN_WIDTH), F32),
            pltpu.VMEM((tb, d), BF16),
            pltpu.VMEM((tb, d), BF16),
            pltpu.VMEM((tb, ATTN_WIDTH), BF16),
            pltpu.VMEM((tb, ATTN_WIDTH), BF16),
            pltpu.VMEM((WINDOW + tb, KV_WIDTH), BF16),
            pltpu.VMEM((WINDOW + tb, KV_WIDTH), BF16),
            pltpu.VMEM((WINDOW + tb, KV_WIDTH), BF16),
            pltpu.VMEM((WINDOW + tb, KV_WIDTH), BF16),
            pltpu.VMEM((HGRN_HEADS, HGRN_HEAD_DIM, HGRN_HEAD_DIM), F32),
            pltpu.VMEM((tb // CHUNK, CHUNK, HGRN_WIDTH), F32),
            pltpu.VMEM((tb, ATTN_WIDTH), BF16),
            pltpu.VMEM((tb, HGRN_WIDTH), BF16),
        ],
        compiler_params=pltpu.CompilerParams(
            dimension_semantics=("arbitrary",),
            vmem_limit_bytes=VMEM_LIMIT_BYTES),
        name=f"hybrid_layer_{layer}",
    )(sinks, x_pairs, x_pairs, cos_blocks, sin_blocks, cos_blocks, sin_blocks, norm_w, w_in, hgrn_norm_w,
      w_up_attn, w_up_hgrn, w_out, lb_logits, sums, masks, final_norm_w)
    return out.reshape(batch, seq, d)


def kernel(x, positions, norm_w, w_in, attn_sinks, hgrn_norm_w, w_up_attn, w_up_hgrn, w_out, lb_logits,
           final_norm_w):
    depth = w_in.shape[0]
    assert depth == DEPTH and x.shape[1] % (2 * TOKEN_BLOCK) == 0 and x.shape[1] % ROPE_BLOCK == 0
    cos_tab, sin_tab = _rope_tables(positions)
    sums_np, masks_np = _hgrn_constants()
    sums = jnp.asarray(sums_np, dtype=BF16)
    masks = jnp.asarray(masks_np, dtype=F32)
    fnw = final_norm_w.reshape(1, D_MODEL)
    for layer in range(depth):
        x = _layer_call(
            x, cos_tab, sin_tab, attn_sinks[layer], norm_w[layer].reshape(1, D_MODEL),
            w_in[layer].astype(BF16), hgrn_norm_w[layer].reshape(1, HGRN_HEAD_DIM),
            w_up_attn[layer].astype(BF16), w_up_hgrn[layer].astype(BF16), w_out[layer].astype(BF16),
            lb_logits, sums, masks, fnw, layer=layer, final=(layer == depth - 1))
    return x
```

```python
import functools

import numpy as np
import jax
import jax.numpy as jnp
from jax import lax
from jax.experimental import pallas as pl
from jax.experimental.pallas import tpu as pltpu

D_MODEL = 1024
DEPTH = 2
N_HEADS = 8
N_KV_HEADS = 2
HEAD_DIM = 64
ATTN_WIDTH = N_HEADS * HEAD_DIM
KV_WIDTH = N_KV_HEADS * HEAD_DIM
WINDOW = 128
ROT_DIM = HEAD_DIM // 4
ROPE_THETA = 500000.0
HGRN_HEADS = 4
HGRN_HEAD_DIM = 128
HGRN_WIDTH = HGRN_HEADS * HGRN_HEAD_DIM
CHUNK = 64
NORM_EPS = 1e-6
MASK_VALUE = -1e30
LOG2_E = 1.4426950408889634
IN_WIDTH = 2 * ATTN_WIDTH + 2 * KV_WIDTH + 4 * HGRN_WIDTH + 2 * D_MODEL

OFF_Q = 0
OFF_K = OFF_Q + ATTN_WIDTH
OFF_V = OFF_K + KV_WIDTH
OFF_Z = OFF_V + KV_WIDTH
OFF_HQ = OFF_Z + ATTN_WIDTH
OFF_HF = OFF_HQ + HGRN_WIDTH
OFF_HI = OFF_HF + HGRN_WIDTH
OFF_HG = OFF_HI + HGRN_WIDTH
OFF_GA = OFF_HG + HGRN_WIDTH
OFF_GH = OFF_GA + D_MODEL

LANES = 128
TOKEN_BLOCK = 256
ROPE_BLOCK = 2048
PROJ_PIECE = 256
TAIL_PIECES = 5
VMEM_LIMIT_BYTES = 56 * 1024 * 1024

LEVELS = (32, 16, 8, 4, 2, 1)

F32 = jnp.float32
BF16 = jnp.bfloat16


def _dot_nn(a, b):
    return lax.dot_general(a, b, (((1,), (0,)), ((), ())), preferred_element_type=F32)


def _dot_nt(a, b):
    return lax.dot_general(a, b, (((1,), (1,)), ((), ())), preferred_element_type=F32)


def _dot_tn(a, b):
    return lax.dot_general(a, b, (((0,), (0,)), ((), ())), preferred_element_type=F32)


def _split3(x):
    hi = x.astype(BF16)
    r1 = x - hi.astype(F32)
    mid = r1.astype(BF16)
    lo = (r1 - mid.astype(F32)).astype(BF16)
    return hi, mid, lo


def _sigmoid(x):
    return 0.5 * jnp.tanh(0.5 * x) + 0.5


def _block_diag2(a):
    zero = jnp.zeros((a.shape[0], LANES), a.dtype)
    top = jnp.concatenate([a[:, :LANES], zero], axis=1)
    bottom = jnp.concatenate([zero, a[:, LANES:]], axis=1)
    return jnp.concatenate([top, bottom], axis=0)


def _hgrn_constants():
    t = np.arange(CHUNK)[:, None]
    u = np.arange(CHUNK)[None, :]
    masks = []
    for h in LEVELS:
        right = (t // h) % 2 == 1
        masks.append((t // (2 * h) == u // (2 * h)) & right & ((u // h) % 2 == 0))
    masks.append(t == u)
    masks = np.stack(masks, axis=0).astype(np.float32)
    tril = (u <= t).astype(np.float32)
    return np.concatenate([tril] * 3, axis=1), np.concatenate([masks] * 2, axis=2)


def _rope_expand_matrix():
    half = ROT_DIM // 2
    e = np.zeros((2 * half, 2 * LANES), np.float32)
    for lane in range(LANES):
        d = lane % HEAD_DIM
        if d < half:
            e[d, lane] = 1.0
            e[half + d, LANES + lane] = -1.0
        elif d < ROT_DIM:
            e[d - half, lane] = 1.0
            e[half + d - half, LANES + lane] = 1.0
    return e


def _rope_table_kernel(pos_ref, invf_ref, expand_ref, cs_ref):
    batch = pos_ref.shape[0]
    lane = lax.broadcasted_iota(jnp.int32, (1, 2 * LANES), 1)
    passthrough = jnp.where((lane < LANES) & (lane % HEAD_DIM >= ROT_DIM), 1.0, 0.0).astype(F32)
    expand = expand_ref[...]
    for b in range(batch):
        pos = pos_ref[b:b + 1, :].astype(F32)
        ang = invf_ref[...] * pos
        cs = jnp.concatenate([jnp.cos(ang), jnp.sin(ang)], axis=0)
        out = None
        for piece in _split3(cs):
            part = _dot_tn(piece, expand)
            out = part if out is None else out + part
        cs_ref[b] = out + passthrough


def _rope_tables(positions):
    batch, seq = positions.shape
    half = ROT_DIM // 2
    inv_freq = jnp.power(ROPE_THETA, -jnp.arange(half, dtype=F32) * (2.0 / ROT_DIM)).reshape(half, 1)
    expand = jnp.asarray(_rope_expand_matrix(), dtype=BF16)
    return pl.pallas_call(
        _rope_table_kernel,
        grid=(seq // ROPE_BLOCK,),
        in_specs=[
            pl.BlockSpec((batch, ROPE_BLOCK), lambda t: (0, t)),
            pl.BlockSpec((half, 1), lambda t: (0, 0)),
            pl.BlockSpec((2 * half, 2 * LANES), lambda t: (0, 0)),
        ],
        out_specs=pl.BlockSpec((batch, ROPE_BLOCK, 2 * LANES), lambda t: (0, t, 0)),
        out_shape=jax.ShapeDtypeStruct((batch, seq, 2 * LANES), F32),
        name="rope_tables",
    )(positions, inv_freq, expand)


def _rms_norm(x, w):
    return x * lax.rsqrt(jnp.mean(x * x, axis=-1, keepdims=True) + NORM_EPS) * w


def _layer_kernel(sinks_ref, xn_ref, xr_ref, cs_a_ref, cs_b_ref, nw_ref, win_ref,
                  hnw_ref, wua_ref, wuh_ref, wo_ref, lbl_ref, tril_ref, masks_ref, fnw_ref, o_ref,
                  proj0_s, proj1_s, h0_s, h1_s, q0_s, q1_s, kb0_s, kb1_s, vb0_s, vb1_s, st_s, g_s, a_s, b_s,
                  *, layer, final, blocks_per_seq):
    j = pl.program_id(0)

    @pl.when(j == 0)
    def _():
        for ref in (proj0_s, h1_s, q0_s, kb0_s, kb1_s, vb0_s, vb1_s, st_s):
            ref[...] = jnp.zeros(ref.shape, ref.dtype)

    phase = functools.partial(
        _layer_phase, sinks_ref, xn_ref, xr_ref, nw_ref, win_ref, hnw_ref, wua_ref, wuh_ref, wo_ref,
        lbl_ref, tril_ref, masks_ref, fnw_ref, o_ref, st_s, g_s, a_s, b_s, layer=layer, final=final)
    even = dict(proj=proj0_s, h=h0_s, q=q0_s, kb=kb0_s, vb=vb0_s)
    odd = dict(proj=proj1_s, h=h1_s, q=q1_s, kb=kb1_s, vb=vb1_s)
    t_a = (2 * j - 2 + blocks_per_seq) % blocks_per_seq
    t_b = (2 * j - 1 + blocks_per_seq) % blocks_per_seq
    phase(cs_a_ref, odd, even, t_idx=t_a, half=0)
    phase(cs_b_ref, even, odd, t_idx=t_b, half=1)


def _layer_phase(sinks_ref, xn_ref, xr_ref, nw_ref, win_ref, hnw_ref, wua_ref, wuh_ref, wo_ref,
                 lbl_ref, tril_ref, masks_ref, fnw_ref, o_ref, st_s, g_s, a_s, b_s, cs_ref, new, old,
                 *, t_idx, half, layer, final):
    tb = cs_ref.shape[1]
    win0 = half * tb
    proj_s, q_s, kb_s, vb_s = old["proj"], old["q"], old["kb"], old["vb"]

    kb_s[0:WINDOW, :] = new["kb"][tb:tb + WINDOW, :]
    vb_s[0:WINDOW, :] = new["vb"][tb:tb + WINDOW, :]

    pieces = iter(range(0, IN_WIDTH, PROJ_PIECE))

    def project(n_pieces):
        for _ in range(n_pieces):
            c0 = next(pieces, None)
            if c0 is not None:
                new["proj"][:, c0:c0 + PROJ_PIECE] = _dot_nn(new["h"][...], win_ref[:, c0:c0 + PROJ_PIECE])

    def rotate_new():
        half = ROT_DIM // 2
        lane = lax.broadcasted_iota(jnp.int32, (1, LANES), 1) % HEAD_DIM
        first_half = lane < half
        cos_t = cs_ref[0, :, :LANES]
        sin_t = cs_ref[0, :, LANES:]

        def rope(tile):
            partner = jnp.where(first_half, pltpu.roll(tile, LANES - half, axis=1),
                                pltpu.roll(tile, half, axis=1))
            return tile * cos_t + partner * sin_t

        scale = HEAD_DIM ** -0.5 * LOG2_E
        for j in range(ATTN_WIDTH // LANES):
            new["q"][:, j * LANES:(j + 1) * LANES] = (
                rope(new["proj"][:, OFF_Q + j * LANES:OFF_Q + (j + 1) * LANES]) * scale).astype(BF16)
        for j in range(KV_WIDTH // LANES):
            new["kb"][WINDOW:, j * LANES:(j + 1) * LANES] = rope(
                new["proj"][:, OFF_K + j * LANES:OFF_K + (j + 1) * LANES]).astype(BF16)
        new["vb"][WINDOW:, :] = new["proj"][:, OFF_V:OFF_V + KV_WIDTH].astype(BF16)

    def normalise_next():
        old["h"][...] = _rms_norm(xn_ref[0, win0:win0 + tb, :], nw_ref[...]).astype(BF16)

    row = lax.broadcasted_iota(jnp.int32, (2 * WINDOW, 2 * WINDOW), 0) % WINDOW
    col = lax.broadcasted_iota(jnp.int32, (2 * WINDOW, 2 * WINDOW), 1)
    band = (col > row) & (col <= row + WINDOW)
    upper_rows = lax.broadcasted_iota(jnp.int32, (2 * WINDOW, 1), 0) < WINDOW
    group = N_HEADS // N_KV_HEADS
    pairs_per_block = N_HEADS // 2
    n_att = (tb // WINDOW) * pairs_per_block
    att = [dict() for _ in range(n_att)]

    def att_scores(k):
        i, pr = divmod(k, pairs_per_block)
        heads = (2 * pr, 2 * pr + 1)
        kv = heads[0] // group
        q2 = jnp.concatenate(
            [q_s[i * WINDOW:(i + 1) * WINDOW, h * HEAD_DIM:(h + 1) * HEAD_DIM] for h in heads], axis=0)
        kh = kb_s[i * WINDOW:(i + 2) * WINDOW, kv * HEAD_DIM:(kv + 1) * HEAD_DIM]
        has_prev = jnp.logical_or(t_idx > 0, i > 0)
        mask = band & jnp.logical_or(col >= WINDOW, has_prev)
        att[k]["s"] = jnp.where(mask, _dot_nt(q2, kh), MASK_VALUE)
        att[k]["sink"] = jnp.where(upper_rows, sinks_ref[heads[0]], sinks_ref[heads[1]]) * LOG2_E

    def att_max(k):
        att[k]["m"] = jnp.maximum(jnp.max(att[k]["s"], axis=-1, keepdims=True), att[k]["sink"])

    def att_probs(k):
        m = att[k]["m"]
        p = jnp.exp2(att[k].pop("s") - m)
        att[k]["rdenom"] = 1.0 / (jnp.sum(p, axis=-1, keepdims=True) + jnp.exp2(att[k].pop("sink") - m))
        att[k]["p"] = p.astype(BF16)

    def att_out(k):
        i, pr = divmod(k, pairs_per_block)
        kv = (2 * pr) // group
        vh = vb_s[i * WINDOW:(i + 2) * WINDOW, kv * HEAD_DIM:(kv + 1) * HEAD_DIM]
        o = _dot_nn(att[k].pop("p"), vh) * att[k].pop("rdenom")
        o = jnp.concatenate([o[:WINDOW], o[WINDOW:]], axis=1)
        z = proj_s[i * WINDOW:(i + 1) * WINDOW, OFF_Z + pr * LANES:OFF_Z + (pr + 1) * LANES]
        a_s[i * WINDOW:(i + 1) * WINDOW, pr * LANES:(pr + 1) * LANES] = (o * (z * _sigmoid(z))).astype(a_s.dtype)

    att_stages = (att_scores, att_max, att_probs, att_out)

    lbl = lbl_ref[...]
    lb_e = jnp.exp(lbl - jnp.max(lbl, axis=0, keepdims=True))
    lb_soft = lb_e / jnp.sum(lb_e, axis=0, keepdims=True)
    lb = jnp.sum(lb_soft[0:layer + 1, :], axis=0, keepdims=True) - lb_soft[0:1, :]
    one_minus_lb = 1.0 - lb
    hnw = hnw_ref[...]
    carry_state = jnp.where(t_idx == 0, 0.0, 1.0).astype(F32)

    n_lvl = len(LEVELS)
    pair_width = 2 * HGRN_HEAD_DIM
    n_chunks = tb // CHUNK
    n_pairs = HGRN_HEADS // 2
    hg = [dict() for _ in range(n_chunks)]

    def hgrn_gates(c):
        rows = slice(c * CHUNK, (c + 1) * CHUNK)
        fx = proj_s[rows, OFF_HF:OFF_HF + HGRN_WIDTH]
        e = jnp.exp(-jnp.abs(fx))
        r = 1.0 / (1.0 + e)
        er = e * r
        pos_f = fx >= 0
        f = lb + one_minus_lb * jnp.where(pos_f, r, er)
        hg[c]["f"] = f
        hg[c]["k"] = one_minus_lb * jnp.where(pos_f, er, r)
        qx = proj_s[rows, OFF_HQ:OFF_HQ + HGRN_WIDTH]
        hg[c]["q"] = qx * _sigmoid(qx)
        logf3 = jnp.concatenate(_split3(jnp.log(f) * LOG2_E), axis=0)
        g_s[c] = _dot_nn(tril_ref[...], logf3)

    def hgrn_decay(c):
        def ref_rows(first, step):
            return jnp.concatenate(
                [jnp.broadcast_to(g_s[c, first + step * i:first + step * i + 1, :], (8, HGRN_WIDTH))
                 for i in range(CHUNK // 8)], axis=0)

        g = g_s[c]
        factors = []
        for h in LEVELS:
            if h >= 4:
                d = jnp.concatenate(
                    [g_s[c, b * 2 * h:(b + 1) * 2 * h, :] - g_s[c, b * 2 * h + h - 1:b * 2 * h + h, :]
                     for b in range(CHUNK // (2 * h))], axis=0)
                factors.append(jnp.exp2(-jnp.abs(d)))
            elif h == 2:
                low = lax.broadcasted_iota(jnp.int32, (CHUNK, 1), 0) % 8 < 4
                factors.append(jnp.exp2(-jnp.abs(g - jnp.where(low, ref_rows(1, 8), ref_rows(5, 8)))))
            else:
                odd = lax.broadcasted_iota(jnp.int32, (CHUNK, 1), 0) % 2 == 1
                factors.append(jnp.where(odd, hg[c].pop("f"), 1.0))
        hg[c]["levels"] = factors
        hg[c]["g_cum"] = jnp.exp2(g)
        hg[c]["g_rest"] = jnp.exp2(g_s[c, CHUNK - 1:CHUNK, :] - g)

    def hgrn_scores(c):
        levels = hg[c].pop("levels")
        hg[c]["scores"] = []
        for pair in range(n_pairs):
            ps = slice(pair * pair_width, (pair + 1) * pair_width)
            q2 = hg[c]["q"][:, ps]
            k2 = hg[c]["k"][:, ps]
            scores = masks_ref[n_lvl] * _dot_nt(q2.astype(BF16), _block_diag2(k2.astype(BF16)))
            for li in range(n_lvl):
                d_l = levels[li][:, ps]
                scores = scores + masks_ref[li] * _dot_nt(
                    (q2 * d_l).astype(BF16), _block_diag2((k2 * d_l).astype(BF16)))
            hg[c]["scores"].append(scores.astype(BF16))

    def hgrn_out(c):
        rows = slice(c * CHUNK, (c + 1) * CHUNK)
        qin = hg[c].pop("q")
        kin = hg[c].pop("k")
        gx = proj_s[rows, OFF_HG:OFF_HG + HGRN_WIDTH]
        gate = gx * _sigmoid(gx)
        for pair in range(n_pairs):
            ps = slice(pair * pair_width, (pair + 1) * pair_width)
            v2 = proj_s[rows, OFF_HI + pair * pair_width:OFF_HI + (pair + 1) * pair_width].astype(BF16)
            o2 = _dot_nn(hg[c]["scores"][pair], _block_diag2(v2))
            g_cum = hg[c]["g_cum"][:, ps]
            g_rest = hg[c]["g_rest"][:, ps]
            qg = (qin[:, ps] * g_cum).astype(BF16)
            kg = (kin[:, ps] * g_rest).astype(BF16)
            for j in range(2):
                head = 2 * pair + j
                ls = slice(j * HGRN_HEAD_DIM, (j + 1) * HGRN_HEAD_DIM)
                hs = slice(head * HGRN_HEAD_DIM, (head + 1) * HGRN_HEAD_DIM)
                state = st_s[head]
                if c == 0:
                    state = state * carry_state
                o = o2[:, ls] + _dot_nt(qg[:, ls], state.astype(BF16))
                st_s[head] = state * g_cum[CHUNK - 1:CHUNK, ls] + _dot_tn(v2[:, ls], kg[:, ls])
                o = o * lax.rsqrt(jnp.mean(o * o, axis=-1, keepdims=True) + NORM_EPS) * hnw
                b_s[rows, hs] = (o * gate[:, hs]).astype(b_s.dtype)
        hg[c].clear()

    hgrn_stages = (hgrn_gates, hgrn_decay, hgrn_scores, hgrn_out)

    def merge(half_idx):
        rows = slice(half_idx * (tb // 2), (half_idx + 1) * (tb // 2))
        win_rows = slice(win0 + rows.start, win0 + rows.stop)
        up_a = _dot_nn(a_s[rows, :], wua_ref[...])
        up_h = _dot_nn(b_s[rows, :], wuh_ref[...])
        merged = (_sigmoid(proj_s[rows, OFF_GA:OFF_GA + D_MODEL]) * up_a
                  + _sigmoid(proj_s[rows, OFF_GH:OFF_GH + D_MODEL]) * up_h)
        y = xr_ref[0, win_rows, :] + _dot_nn(merged.astype(BF16), wo_ref[...])
        if final:
            y = _rms_norm(y, fnw_ref[...])
        o_ref[0, win_rows, :] = y

    n_slots = max(n_att + len(att_stages) - 1, 2 * (n_chunks - 1) + len(hgrn_stages))
    loop_pieces = IN_WIDTH // PROJ_PIECE - TAIL_PIECES
    pieces_per_slot = -(-loop_pieces // n_slots)
    rotate_slot = -(-(OFF_Z // PROJ_PIECE) // pieces_per_slot)
    att_half = -(-(tb // 2) // WINDOW) * pairs_per_block
    merge_slot = max(att_half + len(att_stages) - 1, 2 * (n_chunks // 2 - 1) + len(hgrn_stages))
    for slot in range(n_slots):
        slot_pieces = max(0, min(pieces_per_slot, loop_pieces - slot * pieces_per_slot))
        project(slot_pieces // 2)
        for s, stage in enumerate(att_stages):
            if 0 <= slot - s < n_att:
                stage(slot - s)
        project(slot_pieces - slot_pieces // 2)
        for s, stage in enumerate(hgrn_stages):
            if (slot - s) % 2 == 0 and 0 <= (slot - s) // 2 < n_chunks:
                stage((slot - s) // 2)
        if slot == rotate_slot:
            rotate_new()
        if slot == rotate_slot + 2:
            normalise_next()
        if slot == merge_slot:
            merge(0)
    project(IN_WIDTH // PROJ_PIECE)
    if merge_slot >= n_slots:
        merge(0)
    merge(1)


def _layer_call(x, rope_tab, sinks, norm_w, w_in, hgrn_norm_w, w_up_attn, w_up_hgrn, w_out,
                lb_logits, tril3, masks, final_norm_w, *, layer, final):
    batch, seq, d = x.shape
    tb = TOKEN_BLOCK
    n_blocks = batch * seq // tb
    once = pl.Buffered(1)
    const2 = lambda g: (0, 0)
    const3 = lambda g: (0, 0, 0)
    n_steps = n_blocks // 2 + 1
    pair_next = lambda j: (jnp.minimum(j, n_blocks // 2 - 1), 0, 0)
    pair_old = lambda j: (jnp.maximum(j - 1, 0), 0, 0)
    blk_a = lambda j: (jnp.maximum(2 * j - 1, 0), 0, 0)
    blk_b = lambda j: (jnp.minimum(2 * j, n_blocks - 1), 0, 0)
    x_pairs = x.reshape(n_blocks // 2, 2 * tb, d)
    rope_blocks = rope_tab.reshape(n_blocks, tb, 2 * LANES)
    out = pl.pallas_call(
        functools.partial(_layer_kernel, layer=layer, final=final, blocks_per_seq=seq // tb),
        grid=(n_steps,),
        in_specs=[
            pl.BlockSpec(memory_space=pltpu.SMEM),
            pl.BlockSpec((1, 2 * tb, d), pair_next),
            pl.BlockSpec((1, 2 * tb, d), pair_old),
            pl.BlockSpec((1, tb, 2 * LANES), blk_a),
            pl.BlockSpec((1, tb, 2 * LANES), blk_b),
            pl.BlockSpec((1, d), const2, pipeline_mode=once),
            pl.BlockSpec((d, IN_WIDTH), const2, pipeline_mode=once),
            pl.BlockSpec((1, HGRN_HEAD_DIM), const2, pipeline_mode=once),
            pl.BlockSpec((ATTN_WIDTH, d), const2, pipeline_mode=once),
            pl.BlockSpec((HGRN_WIDTH, d), const2, pipeline_mode=once),
            pl.BlockSpec((d, d), const2, pipeline_mode=once),
            pl.BlockSpec((DEPTH, HGRN_WIDTH), const2, pipeline_mode=once),
            pl.BlockSpec((CHUNK, 3 * CHUNK), const2, pipeline_mode=once),
            pl.BlockSpec((len(LEVELS) + 1, CHUNK, 2 * CHUNK), const3, pipeline_mode=once),
            pl.BlockSpec((1, d), const2, pipeline_mode=once),
        ],
        out_specs=pl.BlockSpec((1, 2 * tb, d), pair_old),
        out_shape=jax.ShapeDtypeStruct((n_blocks // 2, 2 * tb, d), x.dtype),
        scratch_shapes=[
            pltpu.VMEM((tb, IN_WIDTH), F32),
            pltpu.VMEM((tb, IN_WIDTH), F32),
            pltpu.VMEM((tb, d), BF16),
            pltpu.VMEM((tb, d), BF16),
            pltpu.VMEM((tb, ATTN_WIDTH), BF16),
            pltpu.VMEM((tb, ATTN_WIDTH), BF16),
            pltpu.VMEM((WINDOW + tb, KV_WIDTH), BF16),
            pltpu.VMEM((WINDOW + tb, KV_WIDTH), BF16),
            pltpu.VMEM((WINDOW + tb, KV_WIDTH), BF16),
            pltpu.VMEM((WINDOW + tb, KV_WIDTH), BF16),
            pltpu.VMEM((HGRN_HEADS, HGRN_HEAD_DIM, HGRN_HEAD_DIM), F32),
            pltpu.VMEM((tb // CHUNK, CHUNK, HGRN_WIDTH), F32),
            pltpu.VMEM((tb, ATTN_WIDTH), BF16),
            pltpu.VMEM((tb, HGRN_WIDTH), BF16),
        ],
        compiler_params=pltpu.CompilerParams(
            dimension_semantics=("arbitrary",),
            vmem_limit_bytes=VMEM_LIMIT_BYTES),
        name=f"hybrid_layer_{layer}",
    )(sinks, x_pairs, x_pairs, rope_blocks, rope_blocks, norm_w, w_in, hgrn_norm_w,
      w_up_attn, w_up_hgrn, w_out, lb_logits, tril3, masks, final_norm_w)
    return out.reshape(batch, seq, d)


def kernel(x, positions, norm_w, w_in, attn_sinks, hgrn_norm_w, w_up_attn, w_up_hgrn, w_out, lb_logits,
           final_norm_w):
    depth = w_in.shape[0]
    assert depth == DEPTH and x.shape[1] % (2 * TOKEN_BLOCK) == 0 and x.shape[1] % ROPE_BLOCK == 0
    rope_tab = _rope_tables(positions)
    tril3_np, masks_np = _hgrn_constants()
    tril3 = jnp.asarray(tril3_np, dtype=BF16)
    masks = jnp.asarray(masks_np, dtype=F32)
    fnw = final_norm_w.reshape(1, D_MODEL)
    for layer in range(depth):
        x = _layer_call(
            x, rope_tab, attn_sinks[layer], norm_w[layer].reshape(1, D_MODEL),
            w_in[layer].astype(BF16), hgrn_norm_w[layer].reshape(1, HGRN_HEAD_DIM),
            w_up_attn[layer].astype(BF16), w_up_hgrn[layer].astype(BF16), w_out[layer].astype(BF16),
            lb_logits, tril3, masks, fnw, layer=layer, final=(layer == depth - 1))
    return x
```

```python
import functools

import numpy as np
import jax
import jax.numpy as jnp
from jax import lax
from jax.experimental import pallas as pl
from jax.experimental.pallas import tpu as pltpu

D_MODEL = 1024
DEPTH = 2
N_HEADS = 8
N_KV_HEADS = 2
HEAD_DIM = 64
ATTN_WIDTH = N_HEADS * HEAD_DIM
KV_WIDTH = N_KV_HEADS * HEAD_DIM
WINDOW = 128
ROT_DIM = HEAD_DIM // 4
ROPE_THETA = 500000.0
HGRN_HEADS = 4
HGRN_HEAD_DIM = 128
HGRN_WIDTH = HGRN_HEADS * HGRN_HEAD_DIM
CHUNK = 64
NORM_EPS = 1e-6
MASK_VALUE = -1e30
LOG2_E = 1.4426950408889634
IN_WIDTH = 2 * ATTN_WIDTH + 2 * KV_WIDTH + 4 * HGRN_WIDTH + 2 * D_MODEL

OFF_Q = 0
OFF_K = OFF_Q + ATTN_WIDTH
OFF_V = OFF_K + KV_WIDTH
OFF_Z = OFF_V + KV_WIDTH
OFF_HQ = OFF_Z + ATTN_WIDTH
OFF_HF = OFF_HQ + HGRN_WIDTH
OFF_HI = OFF_HF + HGRN_WIDTH
OFF_HG = OFF_HI + HGRN_WIDTH
OFF_GA = OFF_HG + HGRN_WIDTH
OFF_GH = OFF_GA + D_MODEL

LANES = 128
TOKEN_BLOCK = 256
ROPE_BLOCK = 2048
PROJ_PIECE = 256
TAIL_PIECES = 5
VMEM_LIMIT_BYTES = 56 * 1024 * 1024

LEVELS = (32, 16, 8, 4, 2, 1)

F32 = jnp.float32
BF16 = jnp.bfloat16


def _dot_nn(a, b):
    return lax.dot_general(a, b, (((1,), (0,)), ((), ())), preferred_element_type=F32)


def _dot_nt(a, b):
    return lax.dot_general(a, b, (((1,), (1,)), ((), ())), preferred_element_type=F32)


def _dot_tn(a, b):
    return lax.dot_general(a, b, (((0,), (0,)), ((), ())), preferred_element_type=F32)


def _split3(x):
    hi = x.astype(BF16)
    r1 = x - hi.astype(F32)
    mid = r1.astype(BF16)
    lo = (r1 - mid.astype(F32)).astype(BF16)
    return hi, mid, lo


def _sigmoid(x):
    return 0.5 * jnp.tanh(0.5 * x) + 0.5


def _block_diag2(a):
    zero = jnp.zeros((a.shape[0], LANES), a.dtype)
    top = jnp.concatenate([a[:, :LANES], zero], axis=1)
    bottom = jnp.concatenate([zero, a[:, LANES:]], axis=1)
    return jnp.concatenate([top, bottom], axis=0)


def _hgrn_constants():
    t = np.arange(CHUNK)[:, None]
    u = np.arange(CHUNK)[None, :]
    masks = []
    for h in LEVELS:
        right = (t // h) % 2 == 1
        masks.append((t // (2 * h) == u // (2 * h)) & right & ((u // h) % 2 == 0))
    masks.append(t == u)
    masks = np.stack(masks, axis=0).astype(np.float32)
    tril = (u <= t).astype(np.float32)
    return np.concatenate([tril] * 3, axis=1), np.concatenate([masks] * 2, axis=2)


def _rope_expand_matrix():
    half = ROT_DIM // 2
    e = np.zeros((2 * half, 2 * LANES), np.float32)
    for lane in range(LANES):
        d = lane % HEAD_DIM
        if d < half:
            e[d, lane] = 1.0
            e[half + d, LANES + lane] = -1.0
        elif d < ROT_DIM:
            e[d - half, lane] = 1.0
            e[half + d - half, LANES + lane] = 1.0
    return e


def _rope_table_kernel(pos_ref, invf_ref, expand_ref, cs_ref):
    batch = pos_ref.shape[0]
    lane = lax.broadcasted_iota(jnp.int32, (1, 2 * LANES), 1)
    passthrough = jnp.where((lane < LANES) & (lane % HEAD_DIM >= ROT_DIM), 1.0, 0.0).astype(F32)
    expand = expand_ref[...]
    for b in range(batch):
        pos = pos_ref[b:b + 1, :].astype(F32)
        ang = invf_ref[...] * pos
        cs = jnp.concatenate([jnp.cos(ang), jnp.sin(ang)], axis=0)
        out = _dot_tn(jnp.concatenate(_split3(cs), axis=0), expand)
        cs_ref[b] = out + passthrough


def _rope_tables(positions):
    batch, seq = positions.shape
    half = ROT_DIM // 2
    inv_freq = jnp.power(ROPE_THETA, -jnp.arange(half, dtype=F32) * (2.0 / ROT_DIM)).reshape(half, 1)
    expand = jnp.asarray(np.concatenate([_rope_expand_matrix()] * 3, axis=0), dtype=BF16)
    return pl.pallas_call(
        _rope_table_kernel,
        grid=(seq // ROPE_BLOCK,),
        in_specs=[
            pl.BlockSpec((batch, ROPE_BLOCK), lambda t: (0, t)),
            pl.BlockSpec((half, 1), lambda t: (0, 0)),
            pl.BlockSpec((3 * 2 * half, 2 * LANES), lambda t: (0, 0)),
        ],
        out_specs=pl.BlockSpec((batch, ROPE_BLOCK, 2 * LANES), lambda t: (0, t, 0)),
        out_shape=jax.ShapeDtypeStruct((batch, seq, 2 * LANES), F32),
        name="rope_tables",
    )(positions, inv_freq, expand)


def _rms_norm(x, w):
    return x * lax.rsqrt(jnp.mean(x * x, axis=-1, keepdims=True) + NORM_EPS) * w


def _layer_kernel(sinks_ref, xn_ref, xr_ref, cs_a_ref, cs_b_ref, nw_ref, win_ref,
                  hnw_ref, wua_ref, wuh_ref, wo_ref, lbl_ref, tril_ref, masks_ref, fnw_ref, o_ref,
                  proj0_s, proj1_s, h0_s, h1_s, q0_s, q1_s, kb0_s, kb1_s, vb0_s, vb1_s, st_s, g_s, a_s, b_s,
                  *, layer, final, blocks_per_seq):
    j = pl.program_id(0)

    @pl.when(j == 0)
    def _():
        for ref in (proj0_s, h1_s, q0_s, kb0_s, kb1_s, vb0_s, vb1_s, st_s):
            ref[...] = jnp.zeros(ref.shape, ref.dtype)

    phase = functools.partial(
        _layer_phase, sinks_ref, xn_ref, xr_ref, nw_ref, win_ref, hnw_ref, wua_ref, wuh_ref, wo_ref,
        lbl_ref, tril_ref, masks_ref, fnw_ref, o_ref, st_s, g_s, a_s, b_s, layer=layer, final=final)
    even = dict(proj=proj0_s, h=h0_s, q=q0_s, kb=kb0_s, vb=vb0_s)
    odd = dict(proj=proj1_s, h=h1_s, q=q1_s, kb=kb1_s, vb=vb1_s)
    t_a = (2 * j - 2 + blocks_per_seq) % blocks_per_seq
    t_b = (2 * j - 1 + blocks_per_seq) % blocks_per_seq
    phase(cs_a_ref, odd, even, t_idx=t_a, half=0)
    phase(cs_b_ref, even, odd, t_idx=t_b, half=1)


def _layer_phase(sinks_ref, xn_ref, xr_ref, nw_ref, win_ref, hnw_ref, wua_ref, wuh_ref, wo_ref,
                 lbl_ref, tril_ref, masks_ref, fnw_ref, o_ref, st_s, g_s, a_s, b_s, cs_ref, new, old,
                 *, t_idx, half, layer, final):
    tb = cs_ref.shape[1]
    win0 = half * tb
    proj_s, q_s, kb_s, vb_s = old["proj"], old["q"], old["kb"], old["vb"]

    kb_s[0:WINDOW, :] = new["kb"][tb:tb + WINDOW, :]
    vb_s[0:WINDOW, :] = new["vb"][tb:tb + WINDOW, :]

    pieces = iter(range(0, IN_WIDTH, PROJ_PIECE))

    def project(n_pieces):
        for _ in range(n_pieces):
            c0 = next(pieces, None)
            if c0 is not None:
                new["proj"][:, c0:c0 + PROJ_PIECE] = _dot_nn(new["h"][...], win_ref[:, c0:c0 + PROJ_PIECE])

    def rotate_new():
        half = ROT_DIM // 2
        lane = lax.broadcasted_iota(jnp.int32, (1, LANES), 1) % HEAD_DIM
        first_half = lane < half
        cos_t = cs_ref[0, :, :LANES]
        sin_t = cs_ref[0, :, LANES:]

        def rope(tile):
            partner = jnp.where(first_half, pltpu.roll(tile, LANES - half, axis=1),
                                pltpu.roll(tile, half, axis=1))
            return tile * cos_t + partner * sin_t

        scale = HEAD_DIM ** -0.5 * LOG2_E
        for j in range(ATTN_WIDTH // LANES):
            new["q"][:, j * LANES:(j + 1) * LANES] = (
                rope(new["proj"][:, OFF_Q + j * LANES:OFF_Q + (j + 1) * LANES]) * scale).astype(BF16)
        for j in range(KV_WIDTH // LANES):
            new["kb"][WINDOW:, j * LANES:(j + 1) * LANES] = rope(
                new["proj"][:, OFF_K + j * LANES:OFF_K + (j + 1) * LANES]).astype(BF16)
        new["vb"][WINDOW:, :] = new["proj"][:, OFF_V:OFF_V + KV_WIDTH].astype(BF16)

    def normalise_next():
        old["h"][...] = _rms_norm(xn_ref[0, win0:win0 + tb, :], nw_ref[...]).astype(BF16)

    row = lax.broadcasted_iota(jnp.int32, (2 * WINDOW, 2 * WINDOW), 0) % WINDOW
    col = lax.broadcasted_iota(jnp.int32, (2 * WINDOW, 2 * WINDOW), 1)
    band = (col > row) & (col <= row + WINDOW)
    upper_rows = lax.broadcasted_iota(jnp.int32, (2 * WINDOW, 1), 0) < WINDOW
    group = N_HEADS // N_KV_HEADS
    pairs_per_block = N_HEADS // 2
    n_att = (tb // WINDOW) * pairs_per_block
    att = [dict() for _ in range(n_att)]

    def att_scores(k):
        i, pr = divmod(k, pairs_per_block)
        heads = (2 * pr, 2 * pr + 1)
        kv = heads[0] // group
        q2 = jnp.concatenate(
            [q_s[i * WINDOW:(i + 1) * WINDOW, h * HEAD_DIM:(h + 1) * HEAD_DIM] for h in heads], axis=0)
        kh = kb_s[i * WINDOW:(i + 2) * WINDOW, kv * HEAD_DIM:(kv + 1) * HEAD_DIM]
        has_prev = jnp.logical_or(t_idx > 0, i > 0)
        mask = band & jnp.logical_or(col >= WINDOW, has_prev)
        att[k]["s"] = jnp.where(mask, _dot_nt(q2, kh), MASK_VALUE)
        att[k]["sink"] = jnp.where(upper_rows, sinks_ref[heads[0]], sinks_ref[heads[1]]) * LOG2_E

    def att_max(k):
        att[k]["m"] = jnp.maximum(jnp.max(att[k]["s"], axis=-1, keepdims=True), att[k]["sink"])

    def att_probs(k):
        m = att[k]["m"]
        p = jnp.exp2(att[k].pop("s") - m)
        att[k]["rdenom"] = 1.0 / (jnp.sum(p, axis=-1, keepdims=True) + jnp.exp2(att[k].pop("sink") - m))
        att[k]["p"] = p.astype(BF16)

    def att_out(k):
        i, pr = divmod(k, pairs_per_block)
        kv = (2 * pr) // group
        vh = vb_s[i * WINDOW:(i + 2) * WINDOW, kv * HEAD_DIM:(kv + 1) * HEAD_DIM]
        o = _dot_nn(att[k].pop("p"), vh) * att[k].pop("rdenom")
        o = jnp.concatenate([o[:WINDOW], o[WINDOW:]], axis=1)
        z = proj_s[i * WINDOW:(i + 1) * WINDOW, OFF_Z + pr * LANES:OFF_Z + (pr + 1) * LANES]
        a_s[i * WINDOW:(i + 1) * WINDOW, pr * LANES:(pr + 1) * LANES] = (o * (z * _sigmoid(z))).astype(a_s.dtype)

    att_stages = (att_scores, att_max, att_probs, att_out)

    lbl = lbl_ref[...]
    lb_e = jnp.exp(lbl - jnp.max(lbl, axis=0, keepdims=True))
    lb_soft = lb_e / jnp.sum(lb_e, axis=0, keepdims=True)
    lb = jnp.sum(lb_soft[0:layer + 1, :], axis=0, keepdims=True) - lb_soft[0:1, :]
    one_minus_lb = 1.0 - lb
    hnw = hnw_ref[...]
    carry_state = jnp.where(t_idx == 0, 0.0, 1.0).astype(F32)

    n_lvl = len(LEVELS)
    pair_width = 2 * HGRN_HEAD_DIM
    n_chunks = tb // CHUNK
    n_pairs = HGRN_HEADS // 2
    hg = [dict() for _ in range(n_chunks)]

    def hgrn_gates(c):
        rows = slice(c * CHUNK, (c + 1) * CHUNK)
        fx = proj_s[rows, OFF_HF:OFF_HF + HGRN_WIDTH]
        e = jnp.exp(-jnp.abs(fx))
        r = 1.0 / (1.0 + e)
        er = e * r
        pos_f = fx >= 0
        f = lb + one_minus_lb * jnp.where(pos_f, r, er)
        hg[c]["f"] = f
        hg[c]["k"] = one_minus_lb * jnp.where(pos_f, er, r)
        qx = proj_s[rows, OFF_HQ:OFF_HQ + HGRN_WIDTH]
        hg[c]["q"] = qx * _sigmoid(qx)
        logf3 = jnp.concatenate(_split3(jnp.log(f) * LOG2_E), axis=0)
        g_s[c] = _dot_nn(tril_ref[...], logf3)

    def hgrn_decay(c):
        def ref_rows(first, step):
            return jnp.concatenate(
                [jnp.broadcast_to(g_s[c, first + step * i:first + step * i + 1, :], (8, HGRN_WIDTH))
                 for i in range(CHUNK // 8)], axis=0)

        g = g_s[c]
        factors = []
        for h in LEVELS:
            if h >= 4:
                d = jnp.concatenate(
                    [g_s[c, b * 2 * h:(b + 1) * 2 * h, :] - g_s[c, b * 2 * h + h - 1:b * 2 * h + h, :]
                     for b in range(CHUNK // (2 * h))], axis=0)
                factors.append(jnp.exp2(-jnp.abs(d)))
            elif h == 2:
                low = lax.broadcasted_iota(jnp.int32, (CHUNK, 1), 0) % 8 < 4
                factors.append(jnp.exp2(-jnp.abs(g - jnp.where(low, ref_rows(1, 8), ref_rows(5, 8)))))
            else:
                odd = lax.broadcasted_iota(jnp.int32, (CHUNK, 1), 0) % 2 == 1
                factors.append(jnp.where(odd, hg[c].pop("f"), 1.0))
        hg[c]["levels"] = factors
        hg[c]["g_cum"] = jnp.exp2(g)
        hg[c]["g_rest"] = jnp.exp2(g_s[c, CHUNK - 1:CHUNK, :] - g)

    def hgrn_scores(c):
        levels = hg[c].pop("levels")
        hg[c]["scores"] = []
        for pair in range(n_pairs):
            ps = slice(pair * pair_width, (pair + 1) * pair_width)
            q2 = hg[c]["q"][:, ps]
            k2 = hg[c]["k"][:, ps]
            scores = masks_ref[n_lvl] * _dot_nt(q2.astype(BF16), _block_diag2(k2.astype(BF16)))
            for li in range(n_lvl):
                d_l = levels[li][:, ps]
                scores = scores + masks_ref[li] * _dot_nt(
                    (q2 * d_l).astype(BF16), _block_diag2((k2 * d_l).astype(BF16)))
            hg[c]["scores"].append(scores.astype(BF16))

    def hgrn_out(c):
        rows = slice(c * CHUNK, (c + 1) * CHUNK)
        qin = hg[c].pop("q")
        kin = hg[c].pop("k")
        gx = proj_s[rows, OFF_HG:OFF_HG + HGRN_WIDTH]
        gate = gx * _sigmoid(gx)
        for pair in range(n_pairs):
            ps = slice(pair * pair_width, (pair + 1) * pair_width)
            v2 = proj_s[rows, OFF_HI + pair * pair_width:OFF_HI + (pair + 1) * pair_width].astype(BF16)
            o2 = _dot_nn(hg[c]["scores"][pair], _block_diag2(v2))
            g_cum = hg[c]["g_cum"][:, ps]
            g_rest = hg[c]["g_rest"][:, ps]
            qg = (qin[:, ps] * g_cum).astype(BF16)
            kg = (kin[:, ps] * g_rest).astype(BF16)
            for j in range(2):
                head = 2 * pair + j
                ls = slice(j * HGRN_HEAD_DIM, (j + 1) * HGRN_HEAD_DIM)
                hs = slice(head * HGRN_HEAD_DIM, (head + 1) * HGRN_HEAD_DIM)
                state = st_s[head]
                if c == 0:
                    state = state * carry_state
                o = o2[:, ls] + _dot_nt(qg[:, ls], state.astype(BF16))
                st_s[head] = state * g_cum[CHUNK - 1:CHUNK, ls] + _dot_tn(v2[:, ls], kg[:, ls])
                o = o * lax.rsqrt(jnp.mean(o * o, axis=-1, keepdims=True) + NORM_EPS) * hnw
                b_s[rows, hs] = (o * gate[:, hs]).astype(b_s.dtype)
        hg[c].clear()

    hgrn_stages = (hgrn_gates, hgrn_decay, hgrn_scores, hgrn_out)

    def merge(half_idx):
        rows = slice(half_idx * (tb // 2), (half_idx + 1) * (tb // 2))
        win_rows = slice(win0 + rows.start, win0 + rows.stop)
        up_a = _dot_nn(a_s[rows, :], wua_ref[...])
        up_h = _dot_nn(b_s[rows, :], wuh_ref[...])
        merged = (_sigmoid(proj_s[rows, OFF_GA:OFF_GA + D_MODEL]) * up_a
                  + _sigmoid(proj_s[rows, OFF_GH:OFF_GH + D_MODEL]) * up_h)
        y = xr_ref[0, win_rows, :] + _dot_nn(merged.astype(BF16), wo_ref[...])
        if final:
            y = _rms_norm(y, fnw_ref[...])
        o_ref[0, win_rows, :] = y

    n_slots = max(n_att + len(att_stages) - 1, 2 * (n_chunks - 1) + len(hgrn_stages))
    loop_pieces = IN_WIDTH // PROJ_PIECE - TAIL_PIECES
    pieces_per_slot = -(-loop_pieces // n_slots)
    rotate_slot = -(-(OFF_Z // PROJ_PIECE) // pieces_per_slot)
    att_half = -(-(tb // 2) // WINDOW) * pairs_per_block
    merge_slot = max(att_half + len(att_stages) - 1, 2 * (n_chunks // 2 - 1) + len(hgrn_stages))
    for slot in range(n_slots):
        slot_pieces = max(0, min(pieces_per_slot, loop_pieces - slot * pieces_per_slot))
        project(slot_pieces // 2)
        for s, stage in enumerate(att_stages):
            if 0 <= slot - s < n_att:
                stage(slot - s)
        project(slot_pieces - slot_pieces // 2)
        for s, stage in enumerate(hgrn_stages):
            if (slot - s) % 2 == 0 and 0 <= (slot - s) // 2 < n_chunks:
                stage((slot - s) // 2)
        if slot == rotate_slot:
            rotate_new()
        if slot == rotate_slot + 2:
            normalise_next()
        if slot == merge_slot:
            merge(0)
    project(IN_WIDTH // PROJ_PIECE)
    if merge_slot >= n_slots:
        merge(0)
    merge(1)


def _layer_call(x, rope_tab, sinks, norm_w, w_in, hgrn_norm_w, w_up_attn, w_up_hgrn, w_out,
                lb_logits, tril3, masks, final_norm_w, *, layer, final):
    batch, seq, d = x.shape
    tb = TOKEN_BLOCK
    n_blocks = batch * seq // tb
    once = pl.Buffered(1)
    const2 = lambda g: (0, 0)
    const3 = lambda g: (0, 0, 0)
    this_layer = lambda g: (layer, 0, 0)
    n_steps = n_blocks // 2 + 1
    pair_next = lambda j: (jnp.minimum(j, n_blocks // 2 - 1), 0, 0)
    pair_old = lambda j: (jnp.maximum(j - 1, 0), 0, 0)
    blk_a = lambda j: (jnp.maximum(2 * j - 1, 0), 0, 0)
    blk_b = lambda j: (jnp.minimum(2 * j, n_blocks - 1), 0, 0)
    x_pairs = x.reshape(n_blocks // 2, 2 * tb, d)
    rope_blocks = rope_tab.reshape(n_blocks, tb, 2 * LANES)
    out = pl.pallas_call(
        functools.partial(_layer_kernel, layer=layer, final=final, blocks_per_seq=seq // tb),
        grid=(n_steps,),
        in_specs=[
            pl.BlockSpec(memory_space=pltpu.SMEM),
            pl.BlockSpec((1, 2 * tb, d), pair_next),
            pl.BlockSpec((1, 2 * tb, d), pair_old),
            pl.BlockSpec((1, tb, 2 * LANES), blk_a),
            pl.BlockSpec((1, tb, 2 * LANES), blk_b),
            pl.BlockSpec((1, d), const2, pipeline_mode=once),
            pl.BlockSpec((None, d, IN_WIDTH), this_layer, pipeline_mode=once),
            pl.BlockSpec((1, HGRN_HEAD_DIM), const2, pipeline_mode=once),
            pl.BlockSpec((None, ATTN_WIDTH, d), this_layer, pipeline_mode=once),
            pl.BlockSpec((None, HGRN_WIDTH, d), this_layer, pipeline_mode=once),
            pl.BlockSpec((None, d, d), this_layer, pipeline_mode=once),
            pl.BlockSpec((DEPTH, HGRN_WIDTH), const2, pipeline_mode=once),
            pl.BlockSpec((CHUNK, 3 * CHUNK), const2, pipeline_mode=once),
            pl.BlockSpec((len(LEVELS) + 1, CHUNK, 2 * CHUNK), const3, pipeline_mode=once),
            pl.BlockSpec((1, d), const2, pipeline_mode=once),
        ],
        out_specs=pl.BlockSpec((1, 2 * tb, d), pair_old),
        out_shape=jax.ShapeDtypeStruct((n_blocks // 2, 2 * tb, d), x.dtype),
        scratch_shapes=[
            pltpu.VMEM((tb, IN_WIDTH), F32),
            pltpu.VMEM((tb, IN_WIDTH), F32),
            pltpu.VMEM((tb, d), BF16),
            pltpu.VMEM((tb, d), BF16),
            pltpu.VMEM((tb, ATTN_WIDTH), BF16),
            pltpu.VMEM((tb, ATTN_WIDTH), BF16),
            pltpu.VMEM((WINDOW + tb, KV_WIDTH), BF16),
            pltpu.VMEM((WINDOW + tb, KV_WIDTH), BF16),
            pltpu.VMEM((WINDOW + tb, KV_WIDTH), BF16),
            pltpu.VMEM((WINDOW + tb, KV_WIDTH), BF16),
            pltpu.VMEM((HGRN_HEADS, HGRN_HEAD_DIM, HGRN_HEAD_DIM), F32),
            pltpu.VMEM((tb // CHUNK, CHUNK, HGRN_WIDTH), F32),
            pltpu.VMEM((tb, ATTN_WIDTH), BF16),
            pltpu.VMEM((tb, HGRN_WIDTH), BF16),
        ],
        compiler_params=pltpu.CompilerParams(
            dimension_semantics=("arbitrary",),
            vmem_limit_bytes=VMEM_LIMIT_BYTES),
        name=f"hybrid_layer_{layer}",
    )(sinks, x_pairs, x_pairs, rope_blocks, rope_blocks, norm_w, w_in, hgrn_norm_w,
      w_up_attn, w_up_hgrn, w_out, lb_logits, tril3, masks, final_norm_w)
    return out.reshape(batch, seq, d)


def kernel(x, positions, norm_w, w_in, attn_sinks, hgrn_norm_w, w_up_attn, w_up_hgrn, w_out, lb_logits,
           final_norm_w):
    depth = w_in.shape[0]
    assert depth == DEPTH and x.shape[1] % (2 * TOKEN_BLOCK) == 0 and x.shape[1] % ROPE_BLOCK == 0
    rope_tab = _rope_tables(positions)
    tril3_np, masks_np = _hgrn_constants()
    tril3 = jnp.asarray(tril3_np, dtype=BF16)
    masks = jnp.asarray(masks_np, dtype=F32)
    fnw = final_norm_w.reshape(1, D_MODEL)
    w_in_b, w_ua_b, w_uh_b, w_out_b = (w.astype(BF16) for w in (w_in, w_up_attn, w_up_hgrn, w_out))
    for layer in range(depth):
        x = _layer_call(
            x, rope_tab, attn_sinks[layer], norm_w[layer].reshape(1, D_MODEL), w_in_b,
            hgrn_norm_w[layer].reshape(1, HGRN_HEAD_DIM), w_ua_b, w_uh_b, w_out_b,
            lb_logits, tril3, masks, fnw, layer=layer, final=(layer == depth - 1))
    return x
```

```python
import functools

import numpy as np
import jax
import jax.numpy as jnp
from jax import lax
from jax.experimental import pallas as pl
from jax.experimental.pallas import tpu as pltpu

D_MODEL = 1024
DEPTH = 2
N_HEADS = 8
N_KV_HEADS = 2
HEAD_DIM = 64
ATTN_WIDTH = N_HEADS * HEAD_DIM
KV_WIDTH = N_KV_HEADS * HEAD_DIM
WINDOW = 128
ROT_DIM = HEAD_DIM // 4
ROPE_THETA = 500000.0
HGRN_HEADS = 4
HGRN_HEAD_DIM = 128
HGRN_WIDTH = HGRN_HEADS * HGRN_HEAD_DIM
CHUNK = 64
NORM_EPS = 1e-6
MASK_VALUE = -1e30
LOG2_E = 1.4426950408889634
IN_WIDTH = 2 * ATTN_WIDTH + 2 * KV_WIDTH + 4 * HGRN_WIDTH + 2 * D_MODEL

OFF_Q = 0
OFF_K = OFF_Q + ATTN_WIDTH
OFF_V = OFF_K + KV_WIDTH
OFF_Z = OFF_V + KV_WIDTH
OFF_HQ = OFF_Z + ATTN_WIDTH
OFF_HF = OFF_HQ + HGRN_WIDTH
OFF_HI = OFF_HF + HGRN_WIDTH
OFF_HG = OFF_HI + HGRN_WIDTH
OFF_GA = OFF_HG + HGRN_WIDTH
OFF_GH = OFF_GA + D_MODEL

LANES = 128
TOKEN_BLOCK = 256
ROPE_BLOCK = 2048
PROJ_PIECE = 256
TAIL_PIECES = 5
VMEM_LIMIT_BYTES = 56 * 1024 * 1024

assert 2 * HEAD_DIM == LANES and HGRN_HEAD_DIM == LANES and KV_WIDTH % LANES == 0
assert TOKEN_BLOCK % WINDOW == 0 and (TOKEN_BLOCK // 2) % CHUNK == 0 and IN_WIDTH % PROJ_PIECE == 0

LEVELS = (32, 16, 8, 4, 2, 1)

F32 = jnp.float32
BF16 = jnp.bfloat16


def _dot_nn(a, b):
    return lax.dot_general(a, b, (((1,), (0,)), ((), ())), preferred_element_type=F32)


def _dot_nt(a, b):
    return lax.dot_general(a, b, (((1,), (1,)), ((), ())), preferred_element_type=F32)


def _dot_tn(a, b):
    return lax.dot_general(a, b, (((0,), (0,)), ((), ())), preferred_element_type=F32)


def _split3(x):
    hi = x.astype(BF16)
    r1 = x - hi.astype(F32)
    mid = r1.astype(BF16)
    lo = (r1 - mid.astype(F32)).astype(BF16)
    return hi, mid, lo


def _sigmoid(x):
    return 0.5 * jnp.tanh(0.5 * x) + 0.5


def _block_diag2(a):
    zero = jnp.zeros((a.shape[0], LANES), a.dtype)
    top = jnp.concatenate([a[:, :LANES], zero], axis=1)
    bottom = jnp.concatenate([zero, a[:, LANES:]], axis=1)
    return jnp.concatenate([top, bottom], axis=0)


def _hgrn_constants():
    t = np.arange(CHUNK)[:, None]
    u = np.arange(CHUNK)[None, :]
    masks = []
    for h in LEVELS:
        right = (t // h) % 2 == 1
        masks.append((t // (2 * h) == u // (2 * h)) & right & ((u // h) % 2 == 0))
    masks.append(t == u)
    masks = np.stack(masks, axis=0).astype(np.float32)
    tril = (u <= t).astype(np.float32)
    return np.concatenate([tril] * 3, axis=1), np.concatenate([masks] * 2, axis=2)


def _rope_expand_matrix():
    half = ROT_DIM // 2
    e = np.zeros((2 * half, 2 * LANES), np.float32)
    for lane in range(LANES):
        d = lane % HEAD_DIM
        if d < half:
            e[d, lane] = 1.0
            e[half + d, LANES + lane] = -1.0
        elif d < ROT_DIM:
            e[d - half, lane] = 1.0
            e[half + d - half, LANES + lane] = 1.0
    return e


def _rope_table_kernel(pos_ref, invf_ref, expand_ref, cs_ref):
    batch = pos_ref.shape[0]
    lane = lax.broadcasted_iota(jnp.int32, (1, 2 * LANES), 1)
    passthrough = jnp.where((lane < LANES) & (lane % HEAD_DIM >= ROT_DIM), 1.0, 0.0).astype(F32)
    expand = expand_ref[...]
    for b in range(batch):
        pos = pos_ref[b:b + 1, :].astype(F32)
        ang = invf_ref[...] * pos
        cs = jnp.concatenate([jnp.cos(ang), jnp.sin(ang)], axis=0)
        out = _dot_tn(jnp.concatenate(_split3(cs), axis=0), expand)
        cs_ref[b] = out + passthrough


def _rope_tables(positions):
    batch, seq = positions.shape
    half = ROT_DIM // 2
    inv_freq = jnp.power(ROPE_THETA, -jnp.arange(half, dtype=F32) * (2.0 / ROT_DIM)).reshape(half, 1)
    expand = jnp.asarray(np.concatenate([_rope_expand_matrix()] * 3, axis=0), dtype=BF16)
    return pl.pallas_call(
        _rope_table_kernel,
        grid=(seq // ROPE_BLOCK,),
        in_specs=[
            pl.BlockSpec((batch, ROPE_BLOCK), lambda t: (0, t)),
            pl.BlockSpec((half, 1), lambda t: (0, 0)),
            pl.BlockSpec((3 * 2 * half, 2 * LANES), lambda t: (0, 0)),
        ],
        out_specs=pl.BlockSpec((batch, ROPE_BLOCK, 2 * LANES), lambda t: (0, t, 0)),
        out_shape=jax.ShapeDtypeStruct((batch, seq, 2 * LANES), F32),
        name="rope_tables",
    )(positions, inv_freq, expand)


def _rms_norm(x, w):
    return x * lax.rsqrt(jnp.mean(x * x, axis=-1, keepdims=True) + NORM_EPS) * w


def _layer_kernel(sinks_ref, xn_ref, xr_ref, cs_a_ref, cs_b_ref, nw_ref, win_ref,
                  hnw_ref, wua_ref, wuh_ref, wo_ref, lbl_ref, tril_ref, masks_ref, fnw_ref, o_ref,
                  proj0_s, proj1_s, h0_s, h1_s, q0_s, q1_s, kb0_s, kb1_s, vb0_s, vb1_s, st_s, g_s, a_s, b_s,
                  *, layer, final, blocks_per_seq):
    j = pl.program_id(0)

    @pl.when(j == 0)
    def _():
        for ref in (proj0_s, h1_s, q0_s, kb0_s, kb1_s, vb0_s, vb1_s, st_s):
            ref[...] = jnp.zeros(ref.shape, ref.dtype)

    phase = functools.partial(
        _layer_phase, sinks_ref, xn_ref, xr_ref, nw_ref, win_ref, hnw_ref, wua_ref, wuh_ref, wo_ref,
        lbl_ref, tril_ref, masks_ref, fnw_ref, o_ref, st_s, g_s, a_s, b_s, layer=layer, final=final)
    even = dict(proj=proj0_s, h=h0_s, q=q0_s, kb=kb0_s, vb=vb0_s)
    odd = dict(proj=proj1_s, h=h1_s, q=q1_s, kb=kb1_s, vb=vb1_s)
    t_a = (2 * j - 2 + blocks_per_seq) % blocks_per_seq
    t_b = (2 * j - 1 + blocks_per_seq) % blocks_per_seq
    phase(cs_a_ref, odd, even, t_idx=t_a, half=0)
    phase(cs_b_ref, even, odd, t_idx=t_b, half=1)


def _layer_phase(sinks_ref, xn_ref, xr_ref, nw_ref, win_ref, hnw_ref, wua_ref, wuh_ref, wo_ref,
                 lbl_ref, tril_ref, masks_ref, fnw_ref, o_ref, st_s, g_s, a_s, b_s, cs_ref, new, old,
                 *, t_idx, half, layer, final):
    tb = cs_ref.shape[1]
    win0 = half * tb
    proj_s, q_s, kb_s, vb_s = old["proj"], old["q"], old["kb"], old["vb"]

    kb_s[0:WINDOW, :] = new["kb"][tb:tb + WINDOW, :]
    vb_s[0:WINDOW, :] = new["vb"][tb:tb + WINDOW, :]

    pieces = iter(range(0, IN_WIDTH, PROJ_PIECE))

    def project(n_pieces):
        for _ in range(n_pieces):
            c0 = next(pieces, None)
            if c0 is not None:
                new["proj"][:, c0:c0 + PROJ_PIECE] = _dot_nn(new["h"][...], win_ref[:, c0:c0 + PROJ_PIECE])

    def rotate_new():
        half = ROT_DIM // 2
        lane = lax.broadcasted_iota(jnp.int32, (1, LANES), 1) % HEAD_DIM
        first_half = lane < half
        cos_t = cs_ref[0, :, :LANES]
        sin_t = cs_ref[0, :, LANES:]

        def rope(tile):
            partner = jnp.where(first_half, pltpu.roll(tile, LANES - half, axis=1),
                                pltpu.roll(tile, half, axis=1))
            return tile * cos_t + partner * sin_t

        scale = HEAD_DIM ** -0.5 * LOG2_E
        for j in range(ATTN_WIDTH // LANES):
            new["q"][:, j * LANES:(j + 1) * LANES] = (
                rope(new["proj"][:, OFF_Q + j * LANES:OFF_Q + (j + 1) * LANES]) * scale).astype(BF16)
        for j in range(KV_WIDTH // LANES):
            new["kb"][WINDOW:, j * LANES:(j + 1) * LANES] = rope(
                new["proj"][:, OFF_K + j * LANES:OFF_K + (j + 1) * LANES]).astype(BF16)
        new["vb"][WINDOW:, :] = new["proj"][:, OFF_V:OFF_V + KV_WIDTH].astype(BF16)

    def normalise_next():
        old["h"][...] = _rms_norm(xn_ref[0, win0:win0 + tb, :], nw_ref[...]).astype(BF16)

    row = lax.broadcasted_iota(jnp.int32, (2 * WINDOW, 2 * WINDOW), 0) % WINDOW
    col = lax.broadcasted_iota(jnp.int32, (2 * WINDOW, 2 * WINDOW), 1)
    band = (col > row) & (col <= row + WINDOW)
    upper_rows = lax.broadcasted_iota(jnp.int32, (2 * WINDOW, 1), 0) < WINDOW
    group = N_HEADS // N_KV_HEADS
    pairs_per_block = N_HEADS // 2
    n_att = (tb // WINDOW) * pairs_per_block
    att = [dict() for _ in range(n_att)]

    def att_scores(k):
        i, pr = divmod(k, pairs_per_block)
        heads = (2 * pr, 2 * pr + 1)
        kv = heads[0] // group
        q2 = jnp.concatenate(
            [q_s[i * WINDOW:(i + 1) * WINDOW, h * HEAD_DIM:(h + 1) * HEAD_DIM] for h in heads], axis=0)
        kh = kb_s[i * WINDOW:(i + 2) * WINDOW, kv * HEAD_DIM:(kv + 1) * HEAD_DIM]
        has_prev = jnp.logical_or(t_idx > 0, i > 0)
        mask = band & jnp.logical_or(col >= WINDOW, has_prev)
        att[k]["s"] = jnp.where(mask, _dot_nt(q2, kh), MASK_VALUE)
        att[k]["sink"] = jnp.where(upper_rows, sinks_ref[heads[0]], sinks_ref[heads[1]]) * LOG2_E

    def att_max(k):
        att[k]["m"] = jnp.maximum(jnp.max(att[k]["s"], axis=-1, keepdims=True), att[k]["sink"])

    def att_probs(k):
        m = att[k]["m"]
        p = jnp.exp2(att[k].pop("s") - m)
        att[k]["rdenom"] = 1.0 / (jnp.sum(p, axis=-1, keepdims=True) + jnp.exp2(att[k].pop("sink") - m))
        att[k]["p"] = p.astype(BF16)

    def att_out(k):
        i, pr = divmod(k, pairs_per_block)
        kv = (2 * pr) // group
        vh = vb_s[i * WINDOW:(i + 2) * WINDOW, kv * HEAD_DIM:(kv + 1) * HEAD_DIM]
        o = _dot_nn(att[k].pop("p"), vh) * att[k].pop("rdenom")
        o = jnp.concatenate([o[:WINDOW], o[WINDOW:]], axis=1)
        z = proj_s[i * WINDOW:(i + 1) * WINDOW, OFF_Z + pr * LANES:OFF_Z + (pr + 1) * LANES]
        a_s[i * WINDOW:(i + 1) * WINDOW, pr * LANES:(pr + 1) * LANES] = (o * (z * _sigmoid(z))).astype(a_s.dtype)

    att_stages = (att_scores, att_max, att_probs, att_out)

    lbl = lbl_ref[...]
    lb_e = jnp.exp(lbl - jnp.max(lbl, axis=0, keepdims=True))
    lb_soft = lb_e / jnp.sum(lb_e, axis=0, keepdims=True)
    lb = jnp.sum(lb_soft[0:layer + 1, :], axis=0, keepdims=True) - lb_soft[0:1, :]
    one_minus_lb = 1.0 - lb
    hnw = hnw_ref[...]
    carry_state = jnp.where(t_idx == 0, 0.0, 1.0).astype(F32)

    n_lvl = len(LEVELS)
    pair_width = 2 * HGRN_HEAD_DIM
    n_chunks = tb // CHUNK
    n_pairs = HGRN_HEADS // 2
    hg = [dict() for _ in range(n_chunks)]

    def hgrn_gates(c):
        rows = slice(c * CHUNK, (c + 1) * CHUNK)
        fx = proj_s[rows, OFF_HF:OFF_HF + HGRN_WIDTH]
        e = jnp.exp(-jnp.abs(fx))
        r = 1.0 / (1.0 + e)
        er = e * r
        pos_f = fx >= 0
        f = lb + one_minus_lb * jnp.where(pos_f, r, er)
        hg[c]["f"] = f
        hg[c]["k"] = one_minus_lb * jnp.where(pos_f, er, r)
        qx = proj_s[rows, OFF_HQ:OFF_HQ + HGRN_WIDTH]
        hg[c]["q"] = qx * _sigmoid(qx)
        logf3 = jnp.concatenate(_split3(jnp.log(f) * LOG2_E), axis=0)
        g_s[c] = _dot_nn(tril_ref[...], logf3)

    def level_factor(c, h, ps):
        if h >= 4:
            d = jnp.concatenate(
                [g_s[c, b * 2 * h:(b + 1) * 2 * h, ps] - g_s[c, b * 2 * h + h - 1:b * 2 * h + h, ps]
                 for b in range(CHUNK // (2 * h))], axis=0)
            return jnp.exp2(-jnp.abs(d))
        if h == 2:
            def ref_rows(first):
                return jnp.concatenate(
                    [jnp.broadcast_to(g_s[c, first + 8 * i:first + 8 * i + 1, ps], (8, ps.stop - ps.start))
                     for i in range(CHUNK // 8)], axis=0)
            low = lax.broadcasted_iota(jnp.int32, (CHUNK, 1), 0) % 8 < 4
            return jnp.exp2(-jnp.abs(g_s[c, :, ps] - jnp.where(low, ref_rows(1), ref_rows(5))))
        odd = lax.broadcasted_iota(jnp.int32, (CHUNK, 1), 0) % 2 == 1
        return jnp.where(odd, hg[c]["f"][:, ps], 1.0)

    def hgrn_scores(c):
        hg[c]["scores"] = []
        for pair in range(n_pairs):
            ps = slice(pair * pair_width, (pair + 1) * pair_width)
            q2 = hg[c]["q"][:, ps]
            k2 = hg[c]["k"][:, ps]
            scores = masks_ref[n_lvl] * _dot_nt(q2.astype(BF16), _block_diag2(k2.astype(BF16)))
            for li, h in enumerate(LEVELS):
                d_l = level_factor(c, h, ps)
                scores = scores + masks_ref[li] * _dot_nt(
                    (q2 * d_l).astype(BF16), _block_diag2((k2 * d_l).astype(BF16)))
            hg[c]["scores"].append(scores.astype(BF16))

    def hgrn_out(c):
        rows = slice(c * CHUNK, (c + 1) * CHUNK)
        qin = hg[c].pop("q")
        kin = hg[c].pop("k")
        gx = proj_s[rows, OFF_HG:OFF_HG + HGRN_WIDTH]
        gate = gx * _sigmoid(gx)
        for pair in range(n_pairs):
            ps = slice(pair * pair_width, (pair + 1) * pair_width)
            v2 = proj_s[rows, OFF_HI + pair * pair_width:OFF_HI + (pair + 1) * pair_width].astype(BF16)
            o2 = _dot_nn(hg[c]["scores"][pair], _block_diag2(v2))
            g_cum = jnp.exp2(g_s[c, :, ps])
            g_rest = jnp.exp2(g_s[c, CHUNK - 1:CHUNK, ps] - g_s[c, :, ps])
            qg = (qin[:, ps] * g_cum).astype(BF16)
            kg = (kin[:, ps] * g_rest).astype(BF16)
            for j in range(2):
                head = 2 * pair + j
                ls = slice(j * HGRN_HEAD_DIM, (j + 1) * HGRN_HEAD_DIM)
                hs = slice(head * HGRN_HEAD_DIM, (head + 1) * HGRN_HEAD_DIM)
                state = st_s[head]
                if c == 0:
                    state = state * carry_state
                o = o2[:, ls] + _dot_nt(qg[:, ls], state.astype(BF16))
                st_s[head] = state * g_cum[CHUNK - 1:CHUNK, ls] + _dot_tn(v2[:, ls], kg[:, ls])
                o = o * lax.rsqrt(jnp.mean(o * o, axis=-1, keepdims=True) + NORM_EPS) * hnw
                b_s[rows, hs] = (o * gate[:, hs]).astype(b_s.dtype)
        hg[c].clear()

    hgrn_stages = (hgrn_gates, hgrn_scores, hgrn_out)

    def merge(half_idx):
        rows = slice(half_idx * (tb // 2), (half_idx + 1) * (tb // 2))
        win_rows = slice(win0 + rows.start, win0 + rows.stop)
        up_a = _dot_nn(a_s[rows, :], wua_ref[...])
        up_h = _dot_nn(b_s[rows, :], wuh_ref[...])
        merged = (_sigmoid(proj_s[rows, OFF_GA:OFF_GA + D_MODEL]) * up_a
                  + _sigmoid(proj_s[rows, OFF_GH:OFF_GH + D_MODEL]) * up_h)
        y = xr_ref[0, win_rows, :] + _dot_nn(merged.astype(BF16), wo_ref[...])
        if final:
            y = _rms_norm(y, fnw_ref[...])
        o_ref[0, win_rows, :] = y

    n_slots = max(n_att + len(att_stages) - 1, 2 * (n_chunks - 1) + len(hgrn_stages))
    loop_pieces = IN_WIDTH // PROJ_PIECE - TAIL_PIECES
    pieces_per_slot = -(-loop_pieces // n_slots)
    rotate_slot = -(-(OFF_Z // PROJ_PIECE) // pieces_per_slot)
    att_half = -(-(tb // 2) // WINDOW) * pairs_per_block
    merge_slot = max(att_half + len(att_stages) - 1, 2 * (n_chunks // 2 - 1) + len(hgrn_stages))
    for slot in range(n_slots):
        slot_pieces = max(0, min(pieces_per_slot, loop_pieces - slot * pieces_per_slot))
        project(slot_pieces // 2)
        for s, stage in enumerate(att_stages):
            if 0 <= slot - s < n_att:
                stage(slot - s)
        project(slot_pieces - slot_pieces // 2)
        for s, stage in enumerate(hgrn_stages):
            if (slot - s) % 2 == 0 and 0 <= (slot - s) // 2 < n_chunks:
                stage((slot - s) // 2)
        if slot == rotate_slot:
            rotate_new()
        if slot == rotate_slot + 2:
            normalise_next()
        if slot == merge_slot:
            merge(0)
    project(IN_WIDTH // PROJ_PIECE)
    if merge_slot >= n_slots:
        merge(0)
    merge(1)


def _layer_call(x, rope_tab, sinks, norm_w, w_in, hgrn_norm_w, w_up_attn, w_up_hgrn, w_out,
                lb_logits, tril3, masks, final_norm_w, *, layer, final):
    batch, seq, d = x.shape
    tb = TOKEN_BLOCK
    n_blocks = batch * seq // tb
    once = pl.Buffered(1)
    const2 = lambda g: (0, 0)
    const3 = lambda g: (0, 0, 0)
    this_layer = lambda g: (layer, 0, 0)
    n_steps = n_blocks // 2 + 1
    pair_next = lambda j: (jnp.minimum(j, n_blocks // 2 - 1), 0, 0)
    pair_old = lambda j: (jnp.maximum(j - 1, 0), 0, 0)
    blk_a = lambda j: (jnp.maximum(2 * j - 1, 0), 0, 0)
    blk_b = lambda j: (jnp.minimum(2 * j, n_blocks - 1), 0, 0)
    x_pairs = x.reshape(n_blocks // 2, 2 * tb, d)
    rope_blocks = rope_tab.reshape(n_blocks, tb, 2 * LANES)
    out = pl.pallas_call(
        functools.partial(_layer_kernel, layer=layer, final=final, blocks_per_seq=seq // tb),
        grid=(n_steps,),
        in_specs=[
            pl.BlockSpec(memory_space=pltpu.SMEM),
            pl.BlockSpec((1, 2 * tb, d), pair_next),
            pl.BlockSpec((1, 2 * tb, d), pair_old),
            pl.BlockSpec((1, tb, 2 * LANES), blk_a),
            pl.BlockSpec((1, tb, 2 * LANES), blk_b),
            pl.BlockSpec((1, d), const2, pipeline_mode=once),
            pl.BlockSpec((None, d, IN_WIDTH), this_layer, pipeline_mode=once),
            pl.BlockSpec((1, HGRN_HEAD_DIM), const2, pipeline_mode=once),
            pl.BlockSpec((None, ATTN_WIDTH, d), this_layer, pipeline_mode=once),
            pl.BlockSpec((None, HGRN_WIDTH, d), this_layer, pipeline_mode=once),
            pl.BlockSpec((None, d, d), this_layer, pipeline_mode=once),
            pl.BlockSpec((DEPTH, HGRN_WIDTH), const2, pipeline_mode=once),
            pl.BlockSpec((CHUNK, 3 * CHUNK), const2, pipeline_mode=once),
            pl.BlockSpec((len(LEVELS) + 1, CHUNK, 2 * CHUNK), const3, pipeline_mode=once),
            pl.BlockSpec((1, d), const2, pipeline_mode=once),
        ],
        out_specs=pl.BlockSpec((1, 2 * tb, d), pair_old),
        out_shape=jax.ShapeDtypeStruct((n_blocks // 2, 2 * tb, d), x.dtype),
        scratch_shapes=[
            pltpu.VMEM((tb, IN_WIDTH), F32),
            pltpu.VMEM((tb, IN_WIDTH), F32),
            pltpu.VMEM((tb, d), BF16),
            pltpu.VMEM((tb, d), BF16),
            pltpu.VMEM((tb, ATTN_WIDTH), BF16),
            pltpu.VMEM((tb, ATTN_WIDTH), BF16),
            pltpu.VMEM((WINDOW + tb, KV_WIDTH), BF16),
            pltpu.VMEM((WINDOW + tb, KV_WIDTH), BF16),
            pltpu.VMEM((WINDOW + tb, KV_WIDTH), BF16),
            pltpu.VMEM((WINDOW + tb, KV_WIDTH), BF16),
            pltpu.VMEM((HGRN_HEADS, HGRN_HEAD_DIM, HGRN_HEAD_DIM), F32),
            pltpu.VMEM((tb // CHUNK, CHUNK, HGRN_WIDTH), F32),
            pltpu.VMEM((tb, ATTN_WIDTH), BF16),
            pltpu.VMEM((tb, HGRN_WIDTH), BF16),
        ],
        compiler_params=pltpu.CompilerParams(
            dimension_semantics=("arbitrary",),
            vmem_limit_bytes=VMEM_LIMIT_BYTES),
        name=f"hybrid_layer_{layer}",
    )(sinks, x_pairs, x_pairs, rope_blocks, rope_blocks, norm_w, w_in, hgrn_norm_w,
      w_up_attn, w_up_hgrn, w_out, lb_logits, tril3, masks, final_norm_w)
    return out.reshape(batch, seq, d)


def kernel(x, positions, norm_w, w_in, attn_sinks, hgrn_norm_w, w_up_attn, w_up_hgrn, w_out, lb_logits,
           final_norm_w):
    depth = w_in.shape[0]
    assert depth == DEPTH and x.shape[1] % (2 * TOKEN_BLOCK) == 0 and x.shape[1] % ROPE_BLOCK == 0
    rope_tab = _rope_tables(positions)
    tril3_np, masks_np = _hgrn_constants()
    tril3 = jnp.asarray(tril3_np, dtype=BF16)
    masks = jnp.asarray(masks_np, dtype=F32)
    fnw = final_norm_w.reshape(1, D_MODEL)
    w_in_b, w_ua_b, w_uh_b, w_out_b = (w.astype(BF16) for w in (w_in, w_up_attn, w_up_hgrn, w_out))
    for layer in range(depth):
        x = _layer_call(
            x, rope_tab, attn_sinks[layer], norm_w[layer].reshape(1, D_MODEL), w_in_b,
            hgrn_norm_w[layer].reshape(1, HGRN_HEAD_DIM), w_ua_b, w_uh_b, w_out_b,
            lb_logits, tril3, masks, fnw, layer=layer, final=(layer == depth - 1))
    return x
```

```python
import functools

import numpy as np
import jax
import jax.numpy as jnp
from jax import lax
from jax.experimental import pallas as pl
from jax.experimental.pallas import tpu as pltpu

D_MODEL = 1024
DEPTH = 2
N_HEADS = 8
N_KV_HEADS = 2
HEAD_DIM = 64
ATTN_WIDTH = N_HEADS * HEAD_DIM
KV_WIDTH = N_KV_HEADS * HEAD_DIM
WINDOW = 128
ROT_DIM = HEAD_DIM // 4
ROPE_THETA = 500000.0
HGRN_HEADS = 4
HGRN_HEAD_DIM = 128
HGRN_WIDTH = HGRN_HEADS * HGRN_HEAD_DIM
CHUNK = 64
NORM_EPS = 1e-6
MASK_VALUE = -1e30
LOG2_E = 1.4426950408889634
IN_WIDTH = 2 * ATTN_WIDTH + 2 * KV_WIDTH + 4 * HGRN_WIDTH + 2 * D_MODEL

OFF_Q = 0
OFF_K = OFF_Q + ATTN_WIDTH
OFF_V = OFF_K + KV_WIDTH
OFF_Z = OFF_V + KV_WIDTH
OFF_HQ = OFF_Z + ATTN_WIDTH
OFF_HF = OFF_HQ + HGRN_WIDTH
OFF_HI = OFF_HF + HGRN_WIDTH
OFF_HG = OFF_HI + HGRN_WIDTH
OFF_GA = OFF_HG + HGRN_WIDTH
OFF_GH = OFF_GA + D_MODEL

LANES = 128
TOKEN_BLOCK = 256
ROPE_BLOCK = 2048
PROJ_PIECE = 256
TAIL_PIECES = 5
VMEM_LIMIT_BYTES = 56 * 1024 * 1024

assert 2 * HEAD_DIM == LANES and HGRN_HEAD_DIM == LANES and KV_WIDTH % LANES == 0
assert TOKEN_BLOCK % WINDOW == 0 and (TOKEN_BLOCK // 2) % CHUNK == 0 and IN_WIDTH % PROJ_PIECE == 0

LEVELS = (32, 16, 8, 4, 2, 1)

F32 = jnp.float32
BF16 = jnp.bfloat16


def _dot_nn(a, b):
    return lax.dot_general(a, b, (((1,), (0,)), ((), ())), preferred_element_type=F32)


def _dot_nt(a, b):
    return lax.dot_general(a, b, (((1,), (1,)), ((), ())), preferred_element_type=F32)


def _dot_tn(a, b):
    return lax.dot_general(a, b, (((0,), (0,)), ((), ())), preferred_element_type=F32)


def _split3(x):
    hi = x.astype(BF16)
    r1 = x - hi.astype(F32)
    mid = r1.astype(BF16)
    lo = (r1 - mid.astype(F32)).astype(BF16)
    return hi, mid, lo


def _sigmoid(x):
    return 0.5 * jnp.tanh(0.5 * x) + 0.5


def _block_diag2(a):
    zero = jnp.zeros((a.shape[0], LANES), a.dtype)
    top = jnp.concatenate([a[:, :LANES], zero], axis=1)
    bottom = jnp.concatenate([zero, a[:, LANES:]], axis=1)
    return jnp.concatenate([top, bottom], axis=0)


def _hgrn_constants():
    t = np.arange(CHUNK)[:, None]
    u = np.arange(CHUNK)[None, :]
    masks = []
    for h in LEVELS:
        right = (t // h) % 2 == 1
        masks.append((t // (2 * h) == u // (2 * h)) & right & ((u // h) % 2 == 0))
    masks.append(t == u)
    masks = np.stack(masks, axis=0).astype(np.float32)
    tril = (u <= t).astype(np.float32)
    return np.concatenate([tril] * 3, axis=1), np.concatenate([masks] * 2, axis=2)


def _rope_expand_matrix():
    half = ROT_DIM // 2
    e = np.zeros((2 * half, 2 * LANES), np.float32)
    for lane in range(LANES):
        d = lane % HEAD_DIM
        if d < half:
            e[d, lane] = 1.0
            e[half + d, LANES + lane] = -1.0
        elif d < ROT_DIM:
            e[d - half, lane] = 1.0
            e[half + d - half, LANES + lane] = 1.0
    return e


def _rope_table_kernel(pos_ref, invf_ref, expand_ref, cs_ref):
    batch = pos_ref.shape[0]
    lane = lax.broadcasted_iota(jnp.int32, (1, 2 * LANES), 1)
    passthrough = jnp.where((lane < LANES) & (lane % HEAD_DIM >= ROT_DIM), 1.0, 0.0).astype(F32)
    expand = expand_ref[...]
    for b in range(batch):
        pos = pos_ref[b:b + 1, :].astype(F32)
        ang = invf_ref[...] * pos
        cs = jnp.concatenate([jnp.cos(ang), jnp.sin(ang)], axis=0)
        out = _dot_tn(jnp.concatenate(_split3(cs), axis=0), expand)
        cs_ref[b] = out + passthrough


def _rope_tables(positions):
    batch, seq = positions.shape
    half = ROT_DIM // 2
    inv_freq = jnp.power(ROPE_THETA, -jnp.arange(half, dtype=F32) * (2.0 / ROT_DIM)).reshape(half, 1)
    expand = jnp.asarray(np.concatenate([_rope_expand_matrix()] * 3, axis=0), dtype=BF16)
    return pl.pallas_call(
        _rope_table_kernel,
        grid=(seq // ROPE_BLOCK,),
        in_specs=[
            pl.BlockSpec((batch, ROPE_BLOCK), lambda t: (0, t)),
            pl.BlockSpec((half, 1), lambda t: (0, 0)),
            pl.BlockSpec((3 * 2 * half, 2 * LANES), lambda t: (0, 0)),
        ],
        out_specs=pl.BlockSpec((batch, ROPE_BLOCK, 2 * LANES), lambda t: (0, t, 0)),
        out_shape=jax.ShapeDtypeStruct((batch, seq, 2 * LANES), F32),
        name="rope_tables",
    )(positions, inv_freq, expand)


def _rms_norm(x, w):
    return x * lax.rsqrt(jnp.mean(x * x, axis=-1, keepdims=True) + NORM_EPS) * w


def _layer_kernel(sinks_ref, xn_ref, xr_ref, cs_a_ref, cs_b_ref, nw_ref, win_ref,
                  hnw_ref, wua_ref, wuh_ref, wo_ref, lbl_ref, tril_ref, masks_ref, fnw_ref, o_ref,
                  proj0_s, proj1_s, h0_s, h1_s, q0_s, q1_s, kb0_s, kb1_s, vb0_s, vb1_s, st_s, g_s, a_s, b_s,
                  *, layer, final, blocks_per_seq):
    j = pl.program_id(0)

    @pl.when(j == 0)
    def _():
        for ref in (proj0_s, h1_s, q0_s, kb0_s, kb1_s, vb0_s, vb1_s, st_s):
            ref[...] = jnp.zeros(ref.shape, ref.dtype)

    phase = functools.partial(
        _layer_phase, sinks_ref, xn_ref, xr_ref, nw_ref, win_ref, hnw_ref, wua_ref, wuh_ref, wo_ref,
        lbl_ref, tril_ref, masks_ref, fnw_ref, o_ref, st_s, g_s, a_s, b_s, layer=layer, final=final)
    even = dict(proj=proj0_s, h=h0_s, q=q0_s, kb=kb0_s, vb=vb0_s)
    odd = dict(proj=proj1_s, h=h1_s, q=q1_s, kb=kb1_s, vb=vb1_s)
    t_a = (2 * j - 2 + blocks_per_seq) % blocks_per_seq
    t_b = (2 * j - 1 + blocks_per_seq) % blocks_per_seq
    phase(cs_a_ref, odd, even, t_idx=t_a, half=0)
    phase(cs_b_ref, even, odd, t_idx=t_b, half=1)


def _layer_phase(sinks_ref, xn_ref, xr_ref, nw_ref, win_ref, hnw_ref, wua_ref, wuh_ref, wo_ref,
                 lbl_ref, tril_ref, masks_ref, fnw_ref, o_ref, st_s, g_s, a_s, b_s, cs_ref, new, old,
                 *, t_idx, half, layer, final):
    tb = cs_ref.shape[1]
    win0 = half * tb
    proj_s, q_s, kb_s, vb_s = old["proj"], old["q"], old["kb"], old["vb"]

    kb_s[0:WINDOW, :] = new["kb"][tb:tb + WINDOW, :]
    vb_s[0:WINDOW, :] = new["vb"][tb:tb + WINDOW, :]

    pieces = iter(range(0, IN_WIDTH, PROJ_PIECE))

    def project(n_pieces):
        for _ in range(n_pieces):
            c0 = next(pieces, None)
            if c0 is not None:
                new["proj"][:, c0:c0 + PROJ_PIECE] = _dot_nn(new["h"][...], win_ref[:, c0:c0 + PROJ_PIECE])

    def rotate_new():
        half = ROT_DIM // 2
        lane = lax.broadcasted_iota(jnp.int32, (1, LANES), 1) % HEAD_DIM
        first_half = lane < half
        cos_t = cs_ref[0, :, :LANES]
        sin_t = cs_ref[0, :, LANES:]

        def rope(tile):
            partner = jnp.where(first_half, pltpu.roll(tile, LANES - half, axis=1),
                                pltpu.roll(tile, half, axis=1))
            return tile * cos_t + partner * sin_t

        scale = HEAD_DIM ** -0.5 * LOG2_E
        for j in range(ATTN_WIDTH // LANES):
            new["q"][:, j * LANES:(j + 1) * LANES] = (
                rope(new["proj"][:, OFF_Q + j * LANES:OFF_Q + (j + 1) * LANES]) * scale).astype(BF16)
        for j in range(KV_WIDTH // LANES):
            new["kb"][WINDOW:, j * LANES:(j + 1) * LANES] = rope(
                new["proj"][:, OFF_K + j * LANES:OFF_K + (j + 1) * LANES]).astype(BF16)
        new["vb"][WINDOW:, :] = new["proj"][:, OFF_V:OFF_V + KV_WIDTH].astype(BF16)

    def normalise_next():
        old["h"][...] = _rms_norm(xn_ref[0, win0:win0 + tb, :], nw_ref[...]).astype(BF16)

    row = lax.broadcasted_iota(jnp.int32, (2 * WINDOW, 2 * WINDOW), 0) % WINDOW
    col = lax.broadcasted_iota(jnp.int32, (2 * WINDOW, 2 * WINDOW), 1)
    band = (col > row) & (col <= row + WINDOW)
    upper_rows = lax.broadcasted_iota(jnp.int32, (2 * WINDOW, 1), 0) < WINDOW
    group = N_HEADS // N_KV_HEADS
    pairs_per_block = N_HEADS // 2
    n_att = (tb // WINDOW) * pairs_per_block
    att = [dict() for _ in range(n_att)]

    def att_scores(k):
        i, pr = divmod(k, pairs_per_block)
        heads = (2 * pr, 2 * pr + 1)
        kv = heads[0] // group
        q2 = jnp.concatenate(
            [q_s[i * WINDOW:(i + 1) * WINDOW, h * HEAD_DIM:(h + 1) * HEAD_DIM] for h in heads], axis=0)
        kh = kb_s[i * WINDOW:(i + 2) * WINDOW, kv * HEAD_DIM:(kv + 1) * HEAD_DIM]
        has_prev = jnp.logical_or(t_idx > 0, i > 0)
        mask = band & jnp.logical_or(col >= WINDOW, has_prev)
        att[k]["s"] = jnp.where(mask, _dot_nt(q2, kh), MASK_VALUE)
        att[k]["sink"] = jnp.where(upper_rows, sinks_ref[heads[0]], sinks_ref[heads[1]]) * LOG2_E

    def att_max(k):
        att[k]["m"] = jnp.maximum(jnp.max(att[k]["s"], axis=-1, keepdims=True), att[k]["sink"])

    def att_probs(k):
        m = att[k]["m"]
        p = jnp.exp2(att[k].pop("s") - m)
        att[k]["rdenom"] = 1.0 / (jnp.sum(p, axis=-1, keepdims=True) + jnp.exp2(att[k].pop("sink") - m))
        att[k]["p"] = p.astype(BF16)

    def att_out(k):
        i, pr = divmod(k, pairs_per_block)
        kv = (2 * pr) // group
        vh = vb_s[i * WINDOW:(i + 2) * WINDOW, kv * HEAD_DIM:(kv + 1) * HEAD_DIM]
        o = _dot_nn(att[k].pop("p"), vh) * att[k].pop("rdenom")
        o = jnp.concatenate([o[:WINDOW], o[WINDOW:]], axis=1)
        z = proj_s[i * WINDOW:(i + 1) * WINDOW, OFF_Z + pr * LANES:OFF_Z + (pr + 1) * LANES]
        a_s[i * WINDOW:(i + 1) * WINDOW, pr * LANES:(pr + 1) * LANES] = (o * (z * _sigmoid(z))).astype(a_s.dtype)

    att_stages = (att_scores, att_max, att_probs, att_out)

    lbl = lbl_ref[...]
    lb_e = jnp.exp(lbl - jnp.max(lbl, axis=0, keepdims=True))
    lb_soft = lb_e / jnp.sum(lb_e, axis=0, keepdims=True)
    lb = jnp.sum(lb_soft[0:layer + 1, :], axis=0, keepdims=True) - lb_soft[0:1, :]
    one_minus_lb = 1.0 - lb
    hnw = hnw_ref[...]
    carry_state = jnp.where(t_idx == 0, 0.0, 1.0).astype(F32)

    n_lvl = len(LEVELS)
    pair_width = 2 * HGRN_HEAD_DIM
    n_chunks = tb // CHUNK
    n_pairs = HGRN_HEADS // 2
    hg = [dict() for _ in range(n_chunks)]

    def hgrn_gates(c):
        rows = slice(c * CHUNK, (c + 1) * CHUNK)
        fx = proj_s[rows, OFF_HF:OFF_HF + HGRN_WIDTH]
        e = jnp.exp(-jnp.abs(fx))
        r = 1.0 / (1.0 + e)
        er = e * r
        pos_f = fx >= 0
        f = lb + one_minus_lb * jnp.where(pos_f, r, er)
        hg[c]["f"] = f
        hg[c]["k"] = one_minus_lb * jnp.where(pos_f, er, r)
        qx = proj_s[rows, OFF_HQ:OFF_HQ + HGRN_WIDTH]
        hg[c]["q"] = qx * _sigmoid(qx)
        logf3 = jnp.concatenate(_split3(jnp.log(f) * LOG2_E), axis=0)
        g_s[c] = _dot_nn(tril_ref[...], logf3)

    def hgrn_decay(c):
        def ref_rows(first, step):
            return jnp.concatenate(
                [jnp.broadcast_to(g_s[c, first + step * i:first + step * i + 1, :], (8, HGRN_WIDTH))
                 for i in range(CHUNK // 8)], axis=0)

        g = g_s[c]
        factors = []
        for h in LEVELS:
            if h >= 4:
                d = jnp.concatenate(
                    [g_s[c, b * 2 * h:(b + 1) * 2 * h, :] - g_s[c, b * 2 * h + h - 1:b * 2 * h + h, :]
                     for b in range(CHUNK // (2 * h))], axis=0)
                factors.append(jnp.exp2(-jnp.abs(d)))
            elif h == 2:
                low = lax.broadcasted_iota(jnp.int32, (CHUNK, 1), 0) % 8 < 4
                factors.append(jnp.exp2(-jnp.abs(g - jnp.where(low, ref_rows(1, 8), ref_rows(5, 8)))))
            else:
                odd = lax.broadcasted_iota(jnp.int32, (CHUNK, 1), 0) % 2 == 1
                factors.append(jnp.where(odd, hg[c].pop("f"), 1.0))
        hg[c]["levels"] = factors
        hg[c]["g_cum"] = jnp.exp2(g)
        hg[c]["g_rest"] = jnp.exp2(g_s[c, CHUNK - 1:CHUNK, :] - g)

    def hgrn_scores(c):
        levels = hg[c].pop("levels")
        hg[c]["scores"] = []
        for pair in range(n_pairs):
            ps = slice(pair * pair_width, (pair + 1) * pair_width)
            q2 = hg[c]["q"][:, ps]
            k2 = hg[c]["k"][:, ps]
            scores = masks_ref[n_lvl] * _dot_nt(q2.astype(BF16), _block_diag2(k2.astype(BF16)))
            for li in range(n_lvl):
                d_l = levels[li][:, ps]
                scores = scores + masks_ref[li] * _dot_nt(
                    (q2 * d_l).astype(BF16), _block_diag2((k2 * d_l).astype(BF16)))
            hg[c]["scores"].append(scores.astype(BF16))

    def hgrn_out(c):
        rows = slice(c * CHUNK, (c + 1) * CHUNK)
        qin = hg[c].pop("q")
        kin = hg[c].pop("k")
        gx = proj_s[rows, OFF_HG:OFF_HG + HGRN_WIDTH]
        gate = gx * _sigmoid(gx)
        for pair in range(n_pairs):
            ps = slice(pair * pair_width, (pair + 1) * pair_width)
            v2 = proj_s[rows, OFF_HI + pair * pair_width:OFF_HI + (pair + 1) * pair_width].astype(BF16)
            o2 = _dot_nn(hg[c]["scores"][pair], _block_diag2(v2))
            g_cum = hg[c]["g_cum"][:, ps]
            g_rest = hg[c]["g_rest"][:, ps]
            qg = (qin[:, ps] * g_cum).astype(BF16)
            kg = (kin[:, ps] * g_rest).astype(BF16)
            for j in range(2):
                head = 2 * pair + j
                ls = slice(j * HGRN_HEAD_DIM, (j + 1) * HGRN_HEAD_DIM)
                hs = slice(head * HGRN_HEAD_DIM, (head + 1) * HGRN_HEAD_DIM)
                state = st_s[head]
                if c == 0:
                    state = state * carry_state
                o = o2[:, ls] + _dot_nt(qg[:, ls], state.astype(BF16))
                st_s[head] = state * g_cum[CHUNK - 1:CHUNK, ls] + _dot_tn(v2[:, ls], kg[:, ls])
                o = o * lax.rsqrt(jnp.mean(o * o, axis=-1, keepdims=True) + NORM_EPS) * hnw
                b_s[rows, hs] = (o * gate[:, hs]).astype(b_s.dtype)
        hg[c].clear()

    hgrn_stages = (hgrn_gates, hgrn_decay, hgrn_scores, hgrn_out)

    def merge(half_idx):
        rows = slice(half_idx * (tb // 2), (half_idx + 1) * (tb // 2))
        win_rows = slice(win0 + rows.start, win0 + rows.stop)
        up_a = _dot_nn(a_s[rows, :], wua_ref[...])
        up_h = _dot_nn(b_s[rows, :], wuh_ref[...])
        merged = (_sigmoid(proj_s[rows, OFF_GA:OFF_GA + D_MODEL]) * up_a
                  + _sigmoid(proj_s[rows, OFF_GH:OFF_GH + D_MODEL]) * up_h)
        y = xr_ref[0, win_rows, :] + _dot_nn(merged.astype(BF16), wo_ref[...])
        if final:
            y = _rms_norm(y, fnw_ref[...])
        o_ref[0, win_rows, :] = y

    n_slots = max(n_att + len(att_stages) - 1, 2 * (n_chunks - 1) + len(hgrn_stages))
    loop_pieces = IN_WIDTH // PROJ_PIECE - TAIL_PIECES
    pieces_per_slot = -(-loop_pieces // n_slots)
    rotate_slot = -(-(OFF_Z // PROJ_PIECE) // pieces_per_slot)
    att_half = -(-(tb // 2) // WINDOW) * pairs_per_block
    merge_slot = max(att_half + len(att_stages) - 1, 2 * (n_chunks // 2 - 1) + len(hgrn_stages))
    for slot in range(n_slots):
        slot_pieces = max(0, min(pieces_per_slot, loop_pieces - slot * pieces_per_slot))
        project(slot_pieces // 2)
        for s, stage in enumerate(att_stages):
            if 0 <= slot - s < n_att:
                stage(slot - s)
        project(slot_pieces - slot_pieces // 2)
        for s, stage in enumerate(hgrn_stages):
            if (slot - s) % 2 == 0 and 0 <= (slot - s) // 2 < n_chunks:
                stage((slot - s) // 2)
        if slot == rotate_slot:
            rotate_new()
        if slot == rotate_slot + 2:
            normalise_next()
        if slot == merge_slot:
            merge(0)
    project(IN_WIDTH // PROJ_PIECE)
    if merge_slot >= n_slots:
        merge(0)
    merge(1)


def _layer_call(x, rope_tab, sinks, norm_w, w_in, hgrn_norm_w, w_up_attn, w_up_hgrn, w_out,
                lb_logits, tril3, masks, final_norm_w, *, layer, final):
    batch, seq, d = x.shape
    tb = TOKEN_BLOCK
    n_blocks = batch * seq // tb
    once = pl.Buffered(1)
    const2 = lambda g: (0, 0)
    const3 = lambda g: (0, 0, 0)
    this_layer = lambda g: (layer, 0, 0)
    n_steps = n_blocks // 2 + 1
    pair_next = lambda j: (jnp.minimum(j, n_blocks // 2 - 1), 0, 0)
    pair_old = lambda j: (jnp.maximum(j - 1, 0), 0, 0)
    blk_a = lambda j: (jnp.maximum(2 * j - 1, 0), 0, 0)
    blk_b = lambda j: (jnp.minimum(2 * j, n_blocks - 1), 0, 0)
    x_pairs = x.reshape(n_blocks // 2, 2 * tb, d)
    rope_blocks = rope_tab.reshape(n_blocks, tb, 2 * LANES)
    out = pl.pallas_call(
        functools.partial(_layer_kernel, layer=layer, final=final, blocks_per_seq=seq // tb),
        grid=(n_steps,),
        in_specs=[
            pl.BlockSpec(memory_space=pltpu.SMEM),
            pl.BlockSpec((1, 2 * tb, d), pair_next),
            pl.BlockSpec((1, 2 * tb, d), pair_old),
            pl.BlockSpec((1, tb, 2 * LANES), blk_a),
            pl.BlockSpec((1, tb, 2 * LANES), blk_b),
            pl.BlockSpec((1, d), const2, pipeline_mode=once),
            pl.BlockSpec((None, d, IN_WIDTH), this_layer, pipeline_mode=once),
            pl.BlockSpec((1, HGRN_HEAD_DIM), const2, pipeline_mode=once),
            pl.BlockSpec((None, ATTN_WIDTH, d), this_layer, pipeline_mode=once),
            pl.BlockSpec((None, HGRN_WIDTH, d), this_layer, pipeline_mode=once),
            pl.BlockSpec((None, d, d), this_layer, pipeline_mode=once),
            pl.BlockSpec((DEPTH, HGRN_WIDTH), const2, pipeline_mode=once),
            pl.BlockSpec((CHUNK, 3 * CHUNK), const2, pipeline_mode=once),
            pl.BlockSpec((len(LEVELS) + 1, CHUNK, 2 * CHUNK), const3, pipeline_mode=once),
            pl.BlockSpec((1, d), const2, pipeline_mode=once),
        ],
        out_specs=pl.BlockSpec((1, 2 * tb, d), pair_old),
        out_shape=jax.ShapeDtypeStruct((n_blocks // 2, 2 * tb, d), x.dtype),
        scratch_shapes=[
            pltpu.VMEM((tb, IN_WIDTH), F32),
            pltpu.VMEM((tb, IN_WIDTH), F32),
            pltpu.VMEM((tb, d), BF16),
            pltpu.VMEM((tb, d), BF16),
            pltpu.VMEM((tb, ATTN_WIDTH), BF16),
            pltpu.VMEM((tb, ATTN_WIDTH), BF16),
            pltpu.VMEM((WINDOW + tb, KV_WIDTH), BF16),
            pltpu.VMEM((WINDOW + tb, KV_WIDTH), BF16),
            pltpu.VMEM((WINDOW + tb, KV_WIDTH), BF16),
            pltpu.VMEM((WINDOW + tb, KV_WIDTH), BF16),
            pltpu.VMEM((HGRN_HEADS, HGRN_HEAD_DIM, HGRN_HEAD_DIM), F32),
            pltpu.VMEM((tb // CHUNK, CHUNK, HGRN_WIDTH), F32),
            pltpu.VMEM((tb, ATTN_WIDTH), BF16),
            pltpu.VMEM((tb, HGRN_WIDTH), BF16),
        ],
        compiler_params=pltpu.CompilerParams(
            dimension_semantics=("arbitrary",),
            vmem_limit_bytes=VMEM_LIMIT_BYTES),
        name=f"hybrid_layer_{layer}",
    )(sinks, x_pairs, x_pairs, rope_blocks, rope_blocks, norm_w, w_in, hgrn_norm_w,
      w_up_attn, w_up_hgrn, w_out, lb_logits, tril3, masks, final_norm_w)
    return out.reshape(batch, seq, d)


def kernel(x, positions, norm_w, w_in, attn_sinks, hgrn_norm_w, w_up_attn, w_up_hgrn, w_out, lb_logits,
           final_norm_w):
    depth = w_in.shape[0]
    assert depth == DEPTH and x.shape[1] % (2 * TOKEN_BLOCK) == 0 and x.shape[1] % ROPE_BLOCK == 0
    rope_tab = _rope_tables(positions)
    tril3_np, masks_np = _hgrn_constants()
    tril3 = jnp.asarray(tril3_np, dtype=BF16)
    masks = jnp.asarray(masks_np, dtype=F32)
    fnw = final_norm_w.reshape(1, D_MODEL)
    w_in_b, w_ua_b, w_uh_b, w_out_b = (w.astype(BF16) for w in (w_in, w_up_attn, w_up_hgrn, w_out))
    for layer in range(depth):
        x = _layer_call(
            x, rope_tab, attn_sinks[layer], norm_w[layer].reshape(1, D_MODEL), w_in_b,
            hgrn_norm_w[layer].reshape(1, HGRN_HEAD_DIM), w_ua_b, w_uh_b, w_out_b,
            lb_logits, tril3, masks, fnw, layer=layer, final=(layer == depth - 1))
    return x
```

```python
import functools

import numpy as np
import jax
import jax.numpy as jnp
from jax import lax
from jax.experimental import pallas as pl
from jax.experimental.pallas import tpu as pltpu

D_MODEL = 1024
DEPTH = 2
N_HEADS = 8
N_KV_HEADS = 2
HEAD_DIM = 64
ATTN_WIDTH = N_HEADS * HEAD_DIM
KV_WIDTH = N_KV_HEADS * HEAD_DIM
WINDOW = 128
ROT_DIM = HEAD_DIM // 4
ROPE_THETA = 500000.0
HGRN_HEADS = 4
HGRN_HEAD_DIM = 128
HGRN_WIDTH = HGRN_HEADS * HGRN_HEAD_DIM
CHUNK = 64
NORM_EPS = 1e-6
MASK_VALUE = -1e30
LOG2_E = 1.4426950408889634
IN_WIDTH = 2 * ATTN_WIDTH + 2 * KV_WIDTH + 4 * HGRN_WIDTH + 2 * D_MODEL

OFF_Q = 0
OFF_K = OFF_Q + ATTN_WIDTH
OFF_V = OFF_K + KV_WIDTH
OFF_Z = OFF_V + KV_WIDTH
OFF_HQ = OFF_Z + ATTN_WIDTH
OFF_HF = OFF_HQ + HGRN_WIDTH
OFF_HI = OFF_HF + HGRN_WIDTH
OFF_HG = OFF_HI + HGRN_WIDTH
OFF_GA = OFF_HG + HGRN_WIDTH
OFF_GH = OFF_GA + D_MODEL

LANES = 128
TOKEN_BLOCK = 256
ROPE_BLOCK = 2048
PROJ_PIECE = 256
TAIL_PIECES = 5
VMEM_LIMIT_BYTES = 56 * 1024 * 1024

assert 2 * HEAD_DIM == LANES and HGRN_HEAD_DIM == LANES and KV_WIDTH % LANES == 0
assert TOKEN_BLOCK % WINDOW == 0 and (TOKEN_BLOCK // 2) % CHUNK == 0 and IN_WIDTH % PROJ_PIECE == 0

LEVELS = (32, 16, 8, 4, 2, 1)

F32 = jnp.float32
BF16 = jnp.bfloat16


def _dot_nn(a, b):
    return lax.dot_general(a, b, (((1,), (0,)), ((), ())), preferred_element_type=F32)


def _dot_nt(a, b):
    return lax.dot_general(a, b, (((1,), (1,)), ((), ())), preferred_element_type=F32)


def _dot_tn(a, b):
    return lax.dot_general(a, b, (((0,), (0,)), ((), ())), preferred_element_type=F32)


def _split3(x):
    hi = x.astype(BF16)
    r1 = x - hi.astype(F32)
    mid = r1.astype(BF16)
    lo = (r1 - mid.astype(F32)).astype(BF16)
    return hi, mid, lo


def _sigmoid(x):
    return 0.5 * jnp.tanh(0.5 * x) + 0.5


def _block_diag2(a):
    zero = jnp.zeros((a.shape[0], LANES), a.dtype)
    top = jnp.concatenate([a[:, :LANES], zero], axis=1)
    bottom = jnp.concatenate([zero, a[:, LANES:]], axis=1)
    return jnp.concatenate([top, bottom], axis=0)


def _hgrn_constants():
    t = np.arange(CHUNK)[:, None]
    u = np.arange(CHUNK)[None, :]
    masks = []
    for h in LEVELS:
        right = (t // h) % 2 == 1
        masks.append((t // (2 * h) == u // (2 * h)) & right & ((u // h) % 2 == 0))
    masks.append(t == u)
    masks = np.stack(masks, axis=0).astype(np.float32)
    tril = (u <= t).astype(np.float32)
    return np.concatenate([tril] * 3, axis=1), np.concatenate([masks] * 2, axis=2)


def _rope_expand_matrix():
    half = ROT_DIM // 2
    e = np.zeros((2 * half, 2 * LANES), np.float32)
    for lane in range(LANES):
        d = lane % HEAD_DIM
        if d < half:
            e[d, lane] = 1.0
            e[half + d, LANES + lane] = -1.0
        elif d < ROT_DIM:
            e[d - half, lane] = 1.0
            e[half + d - half, LANES + lane] = 1.0
    return e


def _rope_table_kernel(pos_ref, invf_ref, expand_ref, cs_ref):
    batch = pos_ref.shape[0]
    lane = lax.broadcasted_iota(jnp.int32, (1, 2 * LANES), 1)
    passthrough = jnp.where((lane < LANES) & (lane % HEAD_DIM >= ROT_DIM), 1.0, 0.0).astype(F32)
    expand = expand_ref[...]
    for b in range(batch):
        pos = pos_ref[b:b + 1, :].astype(F32)
        ang = invf_ref[...] * pos
        cs = jnp.concatenate([jnp.cos(ang), jnp.sin(ang)], axis=0)
        out = _dot_tn(jnp.concatenate(_split3(cs), axis=0), expand)
        cs_ref[b] = out + passthrough


def _rope_tables(positions):
    batch, seq = positions.shape
    half = ROT_DIM // 2
    inv_freq = jnp.power(ROPE_THETA, -jnp.arange(half, dtype=F32) * (2.0 / ROT_DIM)).reshape(half, 1)
    expand = jnp.asarray(np.concatenate([_rope_expand_matrix()] * 3, axis=0), dtype=BF16)
    return pl.pallas_call(
        _rope_table_kernel,
        grid=(seq // ROPE_BLOCK,),
        in_specs=[
            pl.BlockSpec((batch, ROPE_BLOCK), lambda t: (0, t)),
            pl.BlockSpec((half, 1), lambda t: (0, 0)),
            pl.BlockSpec((3 * 2 * half, 2 * LANES), lambda t: (0, 0)),
        ],
        out_specs=pl.BlockSpec((batch, ROPE_BLOCK, 2 * LANES), lambda t: (0, t, 0)),
        out_shape=jax.ShapeDtypeStruct((batch, seq, 2 * LANES), F32),
        name="rope_tables",
    )(positions, inv_freq, expand)


def _rms_norm(x, w):
    return x * lax.rsqrt(jnp.mean(x * x, axis=-1, keepdims=True) + NORM_EPS) * w


def _layer_kernel(sinks_ref, xn_ref, xr_ref, cs_a_ref, cs_b_ref, nw_ref, win_ref,
                  hnw_ref, wua_ref, wuh_ref, wo_ref, lbl_ref, tril_ref, masks_ref, fnw_ref, o_ref,
                  proj0_s, proj1_s, h0_s, h1_s, q0_s, q1_s, kb0_s, kb1_s, vb0_s, vb1_s, st_s, g_s, a_s, b_s,
                  *, layer, final, blocks_per_seq):
    j = pl.program_id(0)

    @pl.when(j == 0)
    def _():
        for ref in (proj0_s, h1_s, q0_s, kb0_s, kb1_s, vb0_s, vb1_s, st_s):
            ref[...] = jnp.zeros(ref.shape, ref.dtype)

    phase = functools.partial(
        _layer_phase, sinks_ref, xn_ref, xr_ref, nw_ref, win_ref, hnw_ref, wua_ref, wuh_ref, wo_ref,
        lbl_ref, tril_ref, masks_ref, fnw_ref, o_ref, st_s, g_s, a_s, b_s, layer=layer, final=final)
    even = dict(proj=proj0_s, h=h0_s, q=q0_s, kb=kb0_s, vb=vb0_s)
    odd = dict(proj=proj1_s, h=h1_s, q=q1_s, kb=kb1_s, vb=vb1_s)
    t_a = (2 * j - 2 + blocks_per_seq) % blocks_per_seq
    t_b = (2 * j - 1 + blocks_per_seq) % blocks_per_seq
    phase(cs_a_ref, odd, even, t_idx=t_a, half=0)
    phase(cs_b_ref, even, odd, t_idx=t_b, half=1)


def _layer_phase(sinks_ref, xn_ref, xr_ref, nw_ref, win_ref, hnw_ref, wua_ref, wuh_ref, wo_ref,
                 lbl_ref, tril_ref, masks_ref, fnw_ref, o_ref, st_s, g_s, a_s, b_s, cs_ref, new, old,
                 *, t_idx, half, layer, final):
    tb = cs_ref.shape[1]
    win0 = half * tb
    proj_s, q_s, kb_s, vb_s = old["proj"], old["q"], old["kb"], old["vb"]

    kb_s[0:WINDOW, :] = new["kb"][tb:tb + WINDOW, :]
    vb_s[0:WINDOW, :] = new["vb"][tb:tb + WINDOW, :]

    pieces = iter(range(0, IN_WIDTH, PROJ_PIECE))

    def project(n_pieces):
        for _ in range(n_pieces):
            c0 = next(pieces, None)
            if c0 is not None:
                new["proj"][:, c0:c0 + PROJ_PIECE] = _dot_nn(new["h"][...], win_ref[:, c0:c0 + PROJ_PIECE])

    def rotate_new():
        half = ROT_DIM // 2
        lane = lax.broadcasted_iota(jnp.int32, (1, LANES), 1) % HEAD_DIM
        first_half = lane < half
        cos_t = cs_ref[0, :, :LANES]
        sin_t = cs_ref[0, :, LANES:]

        def rope(tile):
            partner = jnp.where(first_half, pltpu.roll(tile, LANES - half, axis=1),
                                pltpu.roll(tile, half, axis=1))
            return tile * cos_t + partner * sin_t

        scale = HEAD_DIM ** -0.5 * LOG2_E
        for j in range(ATTN_WIDTH // LANES):
            new["q"][:, j * LANES:(j + 1) * LANES] = (
                rope(new["proj"][:, OFF_Q + j * LANES:OFF_Q + (j + 1) * LANES]) * scale).astype(BF16)
        for j in range(KV_WIDTH // LANES):
            new["kb"][WINDOW:, j * LANES:(j + 1) * LANES] = rope(
                new["proj"][:, OFF_K + j * LANES:OFF_K + (j + 1) * LANES]).astype(BF16)
        new["vb"][WINDOW:, :] = new["proj"][:, OFF_V:OFF_V + KV_WIDTH].astype(BF16)

    def normalise_next():
        old["h"][...] = _rms_norm(xn_ref[0, win0:win0 + tb, :], nw_ref[...]).astype(BF16)

    row = lax.broadcasted_iota(jnp.int32, (2 * WINDOW, 2 * WINDOW), 0) % WINDOW
    col = lax.broadcasted_iota(jnp.int32, (2 * WINDOW, 2 * WINDOW), 1)
    band = (col > row) & (col <= row + WINDOW)
    upper_rows = lax.broadcasted_iota(jnp.int32, (2 * WINDOW, 1), 0) < WINDOW
    group = N_HEADS // N_KV_HEADS
    pairs_per_block = N_HEADS // 2
    n_att = (tb // WINDOW) * pairs_per_block
    att = [dict() for _ in range(n_att)]

    def att_scores(k):
        i, pr = divmod(k, pairs_per_block)
        heads = (2 * pr, 2 * pr + 1)
        kv = heads[0] // group
        q2 = jnp.concatenate(
            [q_s[i * WINDOW:(i + 1) * WINDOW, h * HEAD_DIM:(h + 1) * HEAD_DIM] for h in heads], axis=0)
        kh = kb_s[i * WINDOW:(i + 2) * WINDOW, kv * HEAD_DIM:(kv + 1) * HEAD_DIM]
        has_prev = jnp.logical_or(t_idx > 0, i > 0)
        mask = band & jnp.logical_or(col >= WINDOW, has_prev)
        att[k]["s"] = jnp.where(mask, _dot_nt(q2, kh), MASK_VALUE)
        att[k]["sink"] = jnp.where(upper_rows, sinks_ref[heads[0]], sinks_ref[heads[1]]) * LOG2_E

    def att_max(k):
        att[k]["m"] = jnp.maximum(jnp.max(att[k]["s"], axis=-1, keepdims=True), att[k]["sink"])

    def att_probs(k):
        m = att[k]["m"]
        p = jnp.exp2(att[k].pop("s") - m)
        att[k]["rdenom"] = 1.0 / (jnp.sum(p, axis=-1, keepdims=True) + jnp.exp2(att[k].pop("sink") - m))
        att[k]["p"] = p.astype(BF16)

    def att_out(k):
        i, pr = divmod(k, pairs_per_block)
        kv = (2 * pr) // group
        vh = vb_s[i * WINDOW:(i + 2) * WINDOW, kv * HEAD_DIM:(kv + 1) * HEAD_DIM]
        o = _dot_nn(att[k].pop("p"), vh) * att[k].pop("rdenom")
        o = jnp.concatenate([o[:WINDOW], o[WINDOW:]], axis=1)
        z = proj_s[i * WINDOW:(i + 1) * WINDOW, OFF_Z + pr * LANES:OFF_Z + (pr + 1) * LANES]
        a_s[i * WINDOW:(i + 1) * WINDOW, pr * LANES:(pr + 1) * LANES] = (o * (z * _sigmoid(z))).astype(a_s.dtype)

    att_stages = (att_scores, att_max, att_probs, att_out)

    lbl = lbl_ref[...]
    lb_e = jnp.exp(lbl - jnp.max(lbl, axis=0, keepdims=True))
    lb_soft = lb_e / jnp.sum(lb_e, axis=0, keepdims=True)
    lb = jnp.sum(lb_soft[0:layer + 1, :], axis=0, keepdims=True) - lb_soft[0:1, :]
    one_minus_lb = 1.0 - lb
    hnw = hnw_ref[...]
    carry_state = jnp.where(t_idx == 0, 0.0, 1.0).astype(F32)

    n_lvl = len(LEVELS)
    pair_width = 2 * HGRN_HEAD_DIM
    n_chunks = tb // CHUNK
    n_pairs = HGRN_HEADS // 2
    hg = [dict() for _ in range(n_chunks)]

    def hgrn_gates(c):
        rows = slice(c * CHUNK, (c + 1) * CHUNK)
        fx = proj_s[rows, OFF_HF:OFF_HF + HGRN_WIDTH]
        e = jnp.exp(-jnp.abs(fx))
        r = 1.0 / (1.0 + e)
        er = e * r
        pos_f = fx >= 0
        f = lb + one_minus_lb * jnp.where(pos_f, r, er)
        hg[c]["f"] = f
        hg[c]["k"] = one_minus_lb * jnp.where(pos_f, er, r)
        qx = proj_s[rows, OFF_HQ:OFF_HQ + HGRN_WIDTH]
        hg[c]["q"] = qx * _sigmoid(qx)
        logf3 = jnp.concatenate(_split3(jnp.log(f) * LOG2_E), axis=0)
        g_s[c] = _dot_nn(tril_ref[...], logf3)

    def hgrn_decay(c):
        def ref_rows(first, step):
            return jnp.concatenate(
                [jnp.broadcast_to(g_s[c, first + step * i:first + step * i + 1, :], (8, HGRN_WIDTH))
                 for i in range(CHUNK // 8)], axis=0)

        g = g_s[c]
        factors = []
        for h in LEVELS:
            if h >= 4:
                d = jnp.concatenate(
                    [g_s[c, b * 2 * h:(b + 1) * 2 * h, :] - g_s[c, b * 2 * h + h - 1:b * 2 * h + h, :]
                     for b in range(CHUNK // (2 * h))], axis=0)
                factors.append(jnp.exp2(-jnp.abs(d)))
            elif h == 2:
                low = lax.broadcasted_iota(jnp.int32, (CHUNK, 1), 0) % 8 < 4
                factors.append(jnp.exp2(-jnp.abs(g - jnp.where(low, ref_rows(1, 8), ref_rows(5, 8)))))
            else:
                odd = lax.broadcasted_iota(jnp.int32, (CHUNK, 1), 0) % 2 == 1
                factors.append(jnp.where(odd, hg[c].pop("f"), 1.0))
        hg[c]["levels"] = factors
        hg[c]["g_cum"] = jnp.exp2(g)
        hg[c]["g_rest"] = jnp.exp2(g_s[c, CHUNK - 1:CHUNK, :] - g)

    def hgrn_scores(c):
        levels = hg[c].pop("levels")
        hg[c]["scores"] = []
        for pair in range(n_pairs):
            ps = slice(pair * pair_width, (pair + 1) * pair_width)
            q2 = hg[c]["q"][:, ps]
            k2 = hg[c]["k"][:, ps]
            scores = masks_ref[n_lvl] * _dot_nt(q2.astype(BF16), _block_diag2(k2.astype(BF16)))
            for li, h in enumerate(LEVELS):
                right = (lax.broadcasted_iota(jnp.int32, (CHUNK, 1), 0) // h) % 2 == 1
                u = (jnp.where(right, q2, k2) * levels[li][:, ps]).astype(BF16)
                scores = scores + masks_ref[li] * _dot_nt(u, _block_diag2(u))
            hg[c]["scores"].append(scores.astype(BF16))

    def hgrn_out(c):
        rows = slice(c * CHUNK, (c + 1) * CHUNK)
        qin = hg[c].pop("q")
        kin = hg[c].pop("k")
        gx = proj_s[rows, OFF_HG:OFF_HG + HGRN_WIDTH]
        gate = gx * _sigmoid(gx)
        for pair in range(n_pairs):
            ps = slice(pair * pair_width, (pair + 1) * pair_width)
            v2 = proj_s[rows, OFF_HI + pair * pair_width:OFF_HI + (pair + 1) * pair_width].astype(BF16)
            o2 = _dot_nn(hg[c]["scores"][pair], _block_diag2(v2))
            g_cum = hg[c]["g_cum"][:, ps]
            g_rest = hg[c]["g_rest"][:, ps]
            qg = (qin[:, ps] * g_cum).astype(BF16)
            kg = (kin[:, ps] * g_rest).astype(BF16)
            for j in range(2):
                head = 2 * pair + j
                ls = slice(j * HGRN_HEAD_DIM, (j + 1) * HGRN_HEAD_DIM)
                hs = slice(head * HGRN_HEAD_DIM, (head + 1) * HGRN_HEAD_DIM)
                state = st_s[head]
                if c == 0:
                    state = state * carry_state
                o = o2[:, ls] + _dot_nt(qg[:, ls], state.astype(BF16))
                st_s[head] = state * g_cum[CHUNK - 1:CHUNK, ls] + _dot_tn(v2[:, ls], kg[:, ls])
                o = o * lax.rsqrt(jnp.mean(o * o, axis=-1, keepdims=True) + NORM_EPS) * hnw
                b_s[rows, hs] = (o * gate[:, hs]).astype(b_s.dtype)
        hg[c].clear()

    hgrn_stages = (hgrn_gates, hgrn_decay, hgrn_scores, hgrn_out)

    def merge(half_idx):
        rows = slice(half_idx * (tb // 2), (half_idx + 1) * (tb // 2))
        win_rows = slice(win0 + rows.start, win0 + rows.stop)
        up_a = _dot_nn(a_s[rows, :], wua_ref[...])
        up_h = _dot_nn(b_s[rows, :], wuh_ref[...])
        merged = (_sigmoid(proj_s[rows, OFF_GA:OFF_GA + D_MODEL]) * up_a
                  + _sigmoid(proj_s[rows, OFF_GH:OFF_GH + D_MODEL]) * up_h)
        y = xr_ref[0, win_rows, :] + _dot_nn(merged.astype(BF16), wo_ref[...])
        if final:
            y = _rms_norm(y, fnw_ref[...])
        o_ref[0, win_rows, :] = y

    n_slots = max(n_att + len(att_stages) - 1, 2 * (n_chunks - 1) + len(hgrn_stages))
    loop_pieces = IN_WIDTH // PROJ_PIECE - TAIL_PIECES
    pieces_per_slot = -(-loop_pieces // n_slots)
    rotate_slot = -(-(OFF_Z // PROJ_PIECE) // pieces_per_slot)
    att_half = -(-(tb // 2) // WINDOW) * pairs_per_block
    merge_slot = max(att_half + len(att_stages) - 1, 2 * (n_chunks // 2 - 1) + len(hgrn_stages))
    for slot in range(n_slots):
        slot_pieces = max(0, min(pieces_per_slot, loop_pieces - slot * pieces_per_slot))
        project(slot_pieces // 2)
        for s, stage in enumerate(att_stages):
            if 0 <= slot - s < n_att:
                stage(slot - s)
        project(slot_pieces - slot_pieces // 2)
        for s, stage in enumerate(hgrn_stages):
            if (slot - s) % 2 == 0 and 0 <= (slot - s) // 2 < n_chunks:
                stage((slot - s) // 2)
        if slot == rotate_slot:
            rotate_new()
        if slot == rotate_slot + 2:
            normalise_next()
        if slot == merge_slot:
            merge(0)
    project(IN_WIDTH // PROJ_PIECE)
    if merge_slot >= n_slots:
        merge(0)
    merge(1)


def _layer_call(x, rope_tab, sinks, norm_w, w_in, hgrn_norm_w, w_up_attn, w_up_hgrn, w_out,
                lb_logits, tril3, masks, final_norm_w, *, layer, final):
    batch, seq, d = x.shape
    tb = TOKEN_BLOCK
    n_blocks = batch * seq // tb
    once = pl.Buffered(1)
    const2 = lambda g: (0, 0)
    const3 = lambda g: (0, 0, 0)
    this_layer = lambda g: (layer, 0, 0)
    n_steps = n_blocks // 2 + 1
    pair_next = lambda j: (jnp.minimum(j, n_blocks // 2 - 1), 0, 0)
    pair_old = lambda j: (jnp.maximum(j - 1, 0), 0, 0)
    blk_a = lambda j: (jnp.maximum(2 * j - 1, 0), 0, 0)
    blk_b = lambda j: (jnp.minimum(2 * j, n_blocks - 1), 0, 0)
    x_pairs = x.reshape(n_blocks // 2, 2 * tb, d)
    rope_blocks = rope_tab.reshape(n_blocks, tb, 2 * LANES)
    out = pl.pallas_call(
        functools.partial(_layer_kernel, layer=layer, final=final, blocks_per_seq=seq // tb),
        grid=(n_steps,),
        in_specs=[
            pl.BlockSpec(memory_space=pltpu.SMEM),
            pl.BlockSpec((1, 2 * tb, d), pair_next),
            pl.BlockSpec((1, 2 * tb, d), pair_old),
            pl.BlockSpec((1, tb, 2 * LANES), blk_a),
            pl.BlockSpec((1, tb, 2 * LANES), blk_b),
            pl.BlockSpec((1, d), const2, pipeline_mode=once),
            pl.BlockSpec((None, d, IN_WIDTH), this_layer, pipeline_mode=once),
            pl.BlockSpec((1, HGRN_HEAD_DIM), const2, pipeline_mode=once),
            pl.BlockSpec((None, ATTN_WIDTH, d), this_layer, pipeline_mode=once),
            pl.BlockSpec((None, HGRN_WIDTH, d), this_layer, pipeline_mode=once),
            pl.BlockSpec((None, d, d), this_layer, pipeline_mode=once),
            pl.BlockSpec((DEPTH, HGRN_WIDTH), const2, pipeline_mode=once),
            pl.BlockSpec((CHUNK, 3 * CHUNK), const2, pipeline_mode=once),
            pl.BlockSpec((len(LEVELS) + 1, CHUNK, 2 * CHUNK), const3, pipeline_mode=once),
            pl.BlockSpec((1, d), const2, pipeline_mode=once),
        ],
        out_specs=pl.BlockSpec((1, 2 * tb, d), pair_old),
        out_shape=jax.ShapeDtypeStruct((n_blocks // 2, 2 * tb, d), x.dtype),
        scratch_shapes=[
            pltpu.VMEM((tb, IN_WIDTH), F32),
            pltpu.VMEM((tb, IN_WIDTH), F32),
            pltpu.VMEM((tb, d), BF16),
            pltpu.VMEM((tb, d), BF16),
            pltpu.VMEM((tb, ATTN_WIDTH), BF16),
            pltpu.VMEM((tb, ATTN_WIDTH), BF16),
            pltpu.VMEM((WINDOW + tb, KV_WIDTH), BF16),
            pltpu.VMEM((WINDOW + tb, KV_WIDTH), BF16),
            pltpu.VMEM((WINDOW + tb, KV_WIDTH), BF16),
            pltpu.VMEM((WINDOW + tb, KV_WIDTH), BF16),
            pltpu.VMEM((HGRN_HEADS, HGRN_HEAD_DIM, HGRN_HEAD_DIM), F32),
            pltpu.VMEM((tb // CHUNK, CHUNK, HGRN_WIDTH), F32),
            pltpu.VMEM((tb, ATTN_WIDTH), BF16),
            pltpu.VMEM((tb, HGRN_WIDTH), BF16),
        ],
        compiler_params=pltpu.CompilerParams(
            dimension_semantics=("arbitrary",),
            vmem_limit_bytes=VMEM_LIMIT_BYTES),
        name=f"hybrid_layer_{layer}",
    )(sinks, x_pairs, x_pairs, rope_blocks, rope_blocks, norm_w, w_in, hgrn_norm_w,
      w_up_attn, w_up_hgrn, w_out, lb_logits, tril3, masks, final_norm_w)
    return out.reshape(batch, seq, d)


def kernel(x, positions, norm_w, w_in, attn_sinks, hgrn_norm_w, w_up_attn, w_up_hgrn, w_out, lb_logits,
           final_norm_w):
    depth = w_in.shape[0]
    assert depth == DEPTH and x.shape[1] % (2 * TOKEN_BLOCK) == 0 and x.shape[1] % ROPE_BLOCK == 0
    rope_tab = _rope_tables(positions)
    tril3_np, masks_np = _hgrn_constants()
    tril3 = jnp.asarray(tril3_np, dtype=BF16)
    masks = jnp.asarray(masks_np, dtype=F32)
    fnw = final_norm_w.reshape(1, D_MODEL)
    w_in_b, w_ua_b, w_uh_b, w_out_b = (w.astype(BF16) for w in (w_in, w_up_attn, w_up_hgrn, w_out))
    for layer in range(depth):
        x = _layer_call(
            x, rope_tab, attn_sinks[layer], norm_w[layer].reshape(1, D_MODEL), w_in_b,
            hgrn_norm_w[layer].reshape(1, HGRN_HEAD_DIM), w_ua_b, w_uh_b, w_out_b,
            lb_logits, tril3, masks, fnw, layer=layer, final=(layer == depth - 1))
    return x
```
